```python
import jax, jax.numpy as jnp
from jax import lax
import numpy as np

D_MODEL = 2048
BATCH = 4
SEQ = 8192
DEPTH = 1
DEC_BATCH = 16
DEC_SEQ = 32
PAST_LEN = 1024

CHUNK = 64
D_MIX = D_MODEL
D_A = D_MIX // 2
D_B = D_MIX - D_A
D_PROJ = 2 * D_A + 2 * D_B
A_HEADS = 16
A_HD = D_A // A_HEADS
CONV_W = 4
LRU_C = 8.0
MLP_CHUNK = 128
B_HD = 128
B_HEADS = D_B // B_HD
MEM_TOKENS = 256
MEM_HEADS = 4
MEM_HD = D_MODEL // MEM_HEADS
N_GROUPS = 4
EXPERTS_PER_GROUP = 8
N_EXPERTS = N_GROUPS * EXPERTS_PER_GROUP
TOP_K = 2
D_EXPERT = D_MODEL // 4
MOE_BLOCK = 128
EPS = 1e-6

kernel_name = 'hymba_rglru_chunkmlp_hmoe_stream_step'


def rmsnorm(x, g):
    xf = x.astype(jnp.float32)
    y = xf * lax.rsqrt(jnp.mean(xf * xf, axis=-1, keepdims=True) + EPS)
    return (y * g.astype(jnp.float32)).astype(x.dtype)


def causal_dwconv(x, buf, w, b):
    t = x.shape[1]
    xp = jnp.concatenate([buf.astype(x.dtype), x], axis=1)
    y = b + sum(xp[:, k:k + t] * w[k] for k in range(CONV_W))
    return y, xp[:, t:]


def _lin_combine(l, r):
    return (l[0] * r[0], r[0] * l[1] + r[1])


def rg_lru(x, h0, w_r, b_r, w_i, b_i, lam, is_first):
    bn, t, _ = x.shape
    xh = x.reshape(bn, t, A_HEADS, A_HD)
    r = jax.nn.sigmoid(jnp.einsum('bthi,hij->bthj', xh, w_r).reshape(bn, t, D_A) + b_r)
    i = jax.nn.sigmoid(jnp.einsum('bthi,hij->bthj', xh, w_i).reshape(bn, t, D_A) + b_i)
    log_a = -LRU_C * r.astype(jnp.float32) * jax.nn.softplus(-lam.astype(jnp.float32))
    a = jnp.exp(log_a)
    mult = jnp.where(is_first[None, :, None], 1.0, jnp.sqrt(-jnp.expm1(2.0 * log_a)))
    u = mult * (i * x).astype(jnp.float32)
    u = u.at[:, 0].add(a[:, 0] * h0.astype(jnp.float32))
    _, h = lax.associative_scan(_lin_combine, (a, u), axis=1)
    return h.astype(x.dtype), h[:, -1]


def chunk_spatial_gate(u, v, w_s, b_s):
    bn, t = v.shape[0], v.shape[1]
    L = min(t, MLP_CHUNK)
    vc = v.reshape(bn, t // L, L, B_HEADS, B_HD)
    ws = jnp.tril(w_s[:, :L, :L])
    z = jnp.einsum('hij,bnjhc->bnihc', ws, vc) + jnp.transpose(b_s[:, :L])[None, None, :, :, None]
    return u * z.reshape(bn, t, D_B)


def mixer_block(x, conv_buf, h0, is_first, g_in, w_in, conv_w, conv_b, w_r, b_r, w_i, b_i,
                lam, v_g, w_s, b_s, g_out_a, g_out_b, w_out):
    bn, t, _ = x.shape
    proj = rmsnorm(x, g_in) @ w_in
    xa, ga, ub, vb = jnp.split(proj, [D_A, 2 * D_A, 2 * D_A + D_B], axis=-1)
    xa_conv, new_buf = causal_dwconv(xa, conv_buf, conv_w, conv_b)
    ha, h_last = rg_lru(xa_conv, h0, w_r, b_r, w_i, b_i, lam, is_first)
    ya = ha * jax.nn.gelu(ga)
    v_n = rmsnorm(jax.nn.gelu(vb).reshape(bn, t, B_HEADS, B_HD), v_g)
    yb = chunk_spatial_gate(jax.nn.gelu(ub), v_n, w_s, b_s)
    y = jnp.concatenate([rmsnorm(ya, g_out_a), rmsnorm(yb, g_out_b)], axis=-1) @ w_out
    return x + y, new_buf, h_last, v_n


def mem_kv(mem, g, w_k, w_v):
    bn, m, _ = mem.shape
    mn = rmsnorm(mem, g)
    k = (mn @ w_k).reshape(bn, m, MEM_HEADS, MEM_HD)
    v = (mn @ w_v).reshape(bn, m, MEM_HEADS, MEM_HD)
    return k, v


def mem_attend(x, k, v, g, w_q, w_o):
    bn, t, d = x.shape
    q = (rmsnorm(x, g) @ w_q).reshape(bn, t, MEM_HEADS, MEM_HD)
    s = jnp.einsum('bthd,bmhd->bhtm', q, k.astype(q.dtype)).astype(jnp.float32) * (MEM_HD ** -0.5)
    p = jax.nn.softmax(s, axis=-1).astype(x.dtype)
    o = jnp.einsum('bhtm,bmhd->bthd', p, v.astype(x.dtype)).reshape(bn, t, d)
    return x + o @ w_o


def grouped_expert_ffn(xt, eid, gate, w1, w3, w2):
    n, d = xt.shape
    a_n = n * TOP_K
    flat_e = eid.reshape(a_n)
    order = jnp.argsort(flat_e)
    sorted_e = flat_e[order]
    token_of = order // TOP_K
    counts = jax.ops.segment_sum(jnp.ones((a_n,), jnp.int32), flat_e, num_segments=N_EXPERTS)
    padded = (counts + MOE_BLOCK - 1) // MOE_BLOCK * MOE_BLOCK
    start = jnp.cumsum(counts) - counts
    pend = jnp.cumsum(padded)
    pstart = pend - padded
    dest = pstart[sorted_e] + (jnp.arange(a_n) - start[sorted_e])
    p_rows = (a_n + N_EXPERTS * (MOE_BLOCK - 1)) // MOE_BLOCK * MOE_BLOCK
    nb = p_rows // MOE_BLOCK
    buf = jnp.zeros((p_rows, d), xt.dtype).at[dest].set(xt[token_of])
    block_start = jnp.arange(nb) * MOE_BLOCK
    block_e = jnp.minimum(jnp.sum(pend[None, :] <= block_start[:, None], axis=1), N_EXPERTS - 1)

    def expert_block(args):
        xb, e = args
        hdn = jax.nn.silu(xb @ w1[e]) * (xb @ w3[e])
        return hdn @ w2[e]

    yb = lax.map(expert_block, (buf.reshape(nb, MOE_BLOCK, d), block_e))
    y_sorted = yb.reshape(p_rows, d)[dest] * gate.reshape(a_n)[order][:, None].astype(xt.dtype)
    return jnp.zeros((n, d), xt.dtype).at[token_of].add(y_sorted)


def hier_moe(x, g, w_rc, b_rc, w_rf, b_rf, w1, w3, w2):
    bn, t, d = x.shape
    xn = rmsnorm(x, g).reshape(bn * t, d)
    n = xn.shape[0]
    pc = jax.nn.softmax((xn @ w_rc).astype(jnp.float32) + b_rc.astype(jnp.float32), axis=-1)
    grp = jnp.argmax(pc, axis=-1)
    p_grp = jnp.take_along_axis(pc, grp[:, None], axis=-1)
    lf = ((xn @ w_rf).astype(jnp.float32) + b_rf.astype(jnp.float32)).reshape(n, N_GROUPS, EXPERTS_PER_GROUP)
    lf = jnp.take_along_axis(lf, grp[:, None, None], axis=1)[:, 0]
    top_l, top_i = lax.top_k(lf, TOP_K)
    gate = p_grp * jax.nn.softmax(top_l, axis=-1)
    eid = grp[:, None] * EXPERTS_PER_GROUP + top_i
    y = grouped_expert_ffn(xn, eid, gate, w1, w3, w2)
    return x + y.reshape(bn, t, d)


def _gain(k, shape):
    return 1.0 + 0.1 * jax.random.normal(k, shape, jnp.float32)


def setup_inputs(seed: int = 0) -> dict:
    key = jax.random.key(seed)
    ks = jax.random.split(key, 40)
    f32 = jnp.float32

    def nrm(k, shape, scale):
        return scale * jax.random.normal(k, shape, f32)

    D = D_MODEL
    s = jax.random.uniform(ks[15], (DEPTH, D_A), f32, 0.9, 0.999) ** (1.0 / LRU_C)
    lam = jnp.log(s) - jnp.log1p(-s)
    return {
        'x_prompt': nrm(ks[0], (BATCH, SEQ, D), 1.0),
        'x_sample': nrm(ks[1], (DEC_BATCH, DEC_SEQ, D), 1.0),
        'mem_prompt': nrm(ks[2], (BATCH, MEM_TOKENS, D), 1.0),
        'state_conv_a': nrm(ks[3], (DEPTH, DEC_BATCH, CONV_W - 1, D_A), 1.0),
        'state_rglru_h': nrm(ks[4], (DEPTH, DEC_BATCH, D_A), 0.5),
        'cache_mem_k': nrm(ks[5], (DEPTH, DEC_BATCH, MEM_TOKENS, MEM_HEADS, MEM_HD), 1.0),
        'cache_mem_v': nrm(ks[6], (DEPTH, DEC_BATCH, MEM_TOKENS, MEM_HEADS, MEM_HD), 1.0),
        'norm_mix': _gain(ks[7], (DEPTH, D)),
        'w_in': nrm(ks[8], (DEPTH, D, D_PROJ), D ** -0.5),
        'conv_a_w': nrm(ks[9], (DEPTH, CONV_W, D_A), CONV_W ** -0.5),
        'conv_a_b': nrm(ks[10], (DEPTH, D_A), 0.02),
        'lru_w_r': nrm(ks[11], (DEPTH, A_HEADS, A_HD, A_HD), A_HD ** -0.5),
        'lru_b_r': nrm(ks[12], (DEPTH, D_A), 0.1),
        'lru_w_i': nrm(ks[13], (DEPTH, A_HEADS, A_HD, A_HD), A_HD ** -0.5),
        'lru_b_i': nrm(ks[14], (DEPTH, D_A), 0.1),
        'lru_lambda': lam,
        'v_norm_g': _gain(ks[16], (DEPTH, B_HEADS, B_HD)),
        'mlp_w_s': nrm(ks[17], (DEPTH, B_HEADS, MLP_CHUNK, MLP_CHUNK), MLP_CHUNK ** -0.5),
        'mlp_b_s': 1.0 + nrm(ks[18], (DEPTH, B_HEADS, MLP_CHUNK), 0.1),
        'out_norm_a': _gain(ks[19], (DEPTH, D_A)),
        'out_norm_b': _gain(ks[20], (DEPTH, D_B)),
        'w_out': nrm(ks[21], (DEPTH, D_MIX, D), D_MIX ** -0.5),
        'norm_mem_q': _gain(ks[22], (DEPTH, D)),
        'norm_mem_kv': _gain(ks[23], (DEPTH, D)),
        'w_mem_q': nrm(ks[24], (DEPTH, D, D), D ** -0.5),
        'w_mem_k': nrm(ks[25], (DEPTH, D, D), D ** -0.5),
        'w_mem_v': nrm(ks[26], (DEPTH, D, D), D ** -0.5),
        'w_mem_o': nrm(ks[27], (DEPTH, D, D), D ** -0.5),
        'norm_ffn': _gain(ks[28], (DEPTH, D)),
        'w_router_c': nrm(ks[29], (DEPTH, D, N_GROUPS), D ** -0.5),
        'b_router_c': nrm(ks[30], (DEPTH, N_GROUPS), 0.01),
        'w_router_f': nrm(ks[31], (DEPTH, D, N_EXPERTS), D ** -0.5),
        'b_router_f': nrm(ks[32], (DEPTH, N_EXPERTS), 0.01),
        'w_exp_1': nrm(ks[33], (DEPTH, N_EXPERTS, D, D_EXPERT), D ** -0.5),
        'w_exp_3': nrm(ks[34], (DEPTH, N_EXPERTS, D, D_EXPERT), D ** -0.5),
        'w_exp_2': nrm(ks[35], (DEPTH, N_EXPERTS, D_EXPERT, D), D_EXPERT ** -0.5),
        'norm_final': _gain(ks[36], (D,)),
    }


def reference(x_prompt, x_sample, mem_prompt, state_conv_a, state_rglru_h, cache_mem_k, cache_mem_v,
              norm_mix, w_in, conv_a_w, conv_a_b, lru_w_r, lru_b_r, lru_w_i, lru_b_i, lru_lambda,
              v_norm_g, mlp_w_s, mlp_b_s, out_norm_a, out_norm_b, w_out,
              norm_mem_q, norm_mem_kv, w_mem_q, w_mem_k, w_mem_v, w_mem_o,
              norm_ffn, w_router_c, b_router_c, w_router_f, b_router_f,
              w_exp_1, w_exp_3, w_exp_2, norm_final):
    bp, tp, _ = x_prompt.shape
    ts = x_sample.shape[1]
    first_p = jnp.arange(tp) == 0
    first_s = (PAST_LEN + jnp.arange(ts)) == 0
    conv0 = jnp.zeros((bp, CONV_W - 1, D_A), x_prompt.dtype)
    h0 = jnp.zeros((bp, D_A), jnp.float32)

    yp, ys = x_prompt, x_sample
    conv_p, h_p, mk_p, mv_p, conv_s, h_s, v_s = [], [], [], [], [], [], []
    for l in range(DEPTH):
        mix_w = (norm_mix[l], w_in[l], conv_a_w[l], conv_a_b[l], lru_w_r[l], lru_b_r[l],
                 lru_w_i[l], lru_b_i[l], lru_lambda[l], v_norm_g[l], mlp_w_s[l], mlp_b_s[l],
                 out_norm_a[l], out_norm_b[l], w_out[l])
        yp, cb_p, hl_p, _ = mixer_block(yp, conv0, h0, first_p, *mix_w)
        ys, cb_s, hl_s, vn_s = mixer_block(ys, state_conv_a[l], state_rglru_h[l], first_s, *mix_w)

        k_p, v_p = mem_kv(mem_prompt, norm_mem_kv[l], w_mem_k[l], w_mem_v[l])
        yp = mem_attend(yp, k_p, v_p, norm_mem_q[l], w_mem_q[l], w_mem_o[l])
        ys = mem_attend(ys, cache_mem_k[l], cache_mem_v[l], norm_mem_q[l], w_mem_q[l], w_mem_o[l])

        moe_w = (norm_ffn[l], w_router_c[l], b_router_c[l], w_router_f[l], b_router_f[l],
                 w_exp_1[l], w_exp_3[l], w_exp_2[l])
        yp = hier_moe(yp, *moe_w)
        ys = hier_moe(ys, *moe_w)

        conv_p.append(cb_p); h_p.append(hl_p); mk_p.append(k_p); mv_p.append(v_p)
        conv_s.append(cb_s); h_s.append(hl_s); v_s.append(vn_s)

    y_prompt = rmsnorm(yp, norm_final)
    y_sample = rmsnorm(ys, norm_final)
    return (y_prompt, y_sample,
            jnp.stack(conv_p), jnp.stack(h_p), jnp.stack(mk_p), jnp.stack(mv_p),
            jnp.stack(conv_s), jnp.stack(h_s), jnp.stack(v_s))
```

```python
import functools

import jax
import jax.numpy as jnp
from jax import lax
from jax.experimental import pallas as pl
from jax.experimental.pallas import tpu as pltpu

EPS = 1e-6
LRU_C = 8.0
CONV_W = 4
A_HEADS = 16
B_HD = 128
MLP_CHUNK = 128
MEM_HEADS = 4
N_GROUPS = 4
EXPERTS_PER_GROUP = 8
N_EXPERTS = N_GROUPS * EXPERTS_PER_GROUP
PAST_LEN = 1024

BF16 = jnp.bfloat16
F32 = jnp.float32

V7X_VMEM_BYTES = 64 * 1024 * 1024
V7X_MXU_DIM = 256
SUBLANES = 8

MIX_ROWS = 256
ROUTE_ROWS = 512
EXPERT_ROWS = 256
VMEM_LIMIT = 56 * 1024 * 1024


def _cparams(sem):
    return pltpu.CompilerParams(dimension_semantics=sem, vmem_limit_bytes=VMEM_LIMIT)


def _rms(x, g):
    return x * lax.rsqrt(jnp.mean(x * x, axis=-1, keepdims=True) + EPS) * g


def _dot(a, b):
    return jnp.dot(a, b, preferred_element_type=F32)


def _dot_nt(a, b):
    return lax.dot_general(a, b, (((1,), (1,)), ((), ())), preferred_element_type=F32)


def _full(shape):
    nd = len(shape)
    return pl.BlockSpec(shape, lambda *_: (0,) * nd)


def _mixer_kernel(x_ref, conv0_ref, h0_ref, gin_ref, win_ref, cw_ref, cb_ref, wr_ref, br_ref,
                  wi_ref, bi_ref, lam_ref, vg_ref, ws_ref, bst_ref, goa_ref, gob_ref, wout_ref,
                  *rest, bt, tt, chunk, pos0, emit_vn):
    if emit_vn:
        y_ref, convo_ref, ho_ref, vn_ref = rest[:4]
        scratch = rest[4:]
    else:
        y_ref, convo_ref, ho_ref = rest[:3]
        vn_ref = None
        scratch = rest[3:]
    xa_buf, a_buf, u_buf, h_buf, z_buf, hc_ref = scratch
    t = pl.program_id(1)
    d = x_ref.shape[-1]
    d_a = a_buf.shape[-1]
    d_b = z_buf.shape[-1]
    m = bt * tt

    @pl.when(t == 0)
    def _():
        xa_buf[:, 5:8, :] = conv0_ref[...]
        hc_ref[...] = h0_ref[...]

    x = x_ref[...].reshape(m, d)
    xn = _rms(x, gin_ref[...]).astype(BF16)
    xa = _dot(xn, win_ref[:, 0:d_a])
    ga = _dot(xn, win_ref[:, d_a:2 * d_a])
    ub = _dot(xn, win_ref[:, 2 * d_a:2 * d_a + d_b])
    vb = _dot(xn, win_ref[:, 2 * d_a + d_b:])

    xa_buf[:, 8:8 + tt, :] = xa.reshape(bt, tt, d_a)
    xc = cb_ref[...][None]
    for k in range(CONV_W):
        xc = xc + xa_buf[:, 5 + k:5 + k + tt, :] * cw_ref[k:k + 1, :][None]
    tail = xa_buf[:, tt + 5:tt + 8, :]
    convo_ref[...] = tail
    xa_buf[:, 5:8, :] = tail
    xc = xc.reshape(m, d_a)

    xcb = xc.astype(BF16)
    gw = V7X_MXU_DIM
    r_pre = jnp.concatenate([_dot(xcb[:, g * gw:(g + 1) * gw], wr_ref[g]) for g in range(d_a // gw)], axis=-1)
    i_pre = jnp.concatenate([_dot(xcb[:, g * gw:(g + 1) * gw], wi_ref[g]) for g in range(d_a // gw)], axis=-1)
    r = jax.nn.sigmoid(r_pre + br_ref[...])
    ig = jax.nn.sigmoid(i_pre + bi_ref[...])
    nl = -lam_ref[...]
    softplus = jnp.maximum(nl, 0.0) + jnp.log1p(jnp.exp(-jnp.abs(nl)))
    log_a = -LRU_C * r * softplus
    a = jnp.exp(log_a)
    mult = jnp.sqrt(-jnp.tanh(log_a) * (1.0 + a * a))
    if pos0 == 0:
        row = lax.broadcasted_iota(jnp.int32, (m, d_a), 0)
        is_first = jnp.logical_and(t == 0, (row % tt) == 0)
        mult = jnp.where(is_first, 1.0, mult)
    u = mult * (ig * xc)
    a_buf[...] = a.reshape(bt, tt, d_a)
    u_buf[...] = u.reshape(bt, tt, d_a)

    sub = lax.broadcasted_iota(jnp.int32, (SUBLANES, d_a), 0)
    for b in range(bt):
        def body(j, h, b=b):
            r0 = pl.multiple_of(j * SUBLANES, SUBLANES)
            av = a_buf[b, pl.ds(r0, SUBLANES), :]
            uv = u_buf[b, pl.ds(r0, SUBLANES), :]
            for s in (1, 2, 4):
                a_sh = jnp.where(sub >= s, pltpu.roll(av, s, 0), 1.0)
                u_sh = jnp.where(sub >= s, pltpu.roll(uv, s, 0), 0.0)
                uv = av * u_sh + uv
                av = av * a_sh
            hv = av * h + uv
            h_buf[b, pl.ds(r0, SUBLANES), :] = hv
            return hv[SUBLANES - 1:SUBLANES, :]
        hc_ref[b] = lax.fori_loop(0, tt // SUBLANES, body, hc_ref[b])
    ho_ref[...] = hc_ref[...]

    ya = h_buf[...].reshape(m, d_a) * jax.nn.gelu(ga)
    ya_n = _rms(ya, goa_ref[...]).astype(BF16)

    gv = jax.nn.gelu(vb)
    heads = d_b // B_HD
    ri = lax.broadcasted_iota(jnp.int32, (chunk, chunk), 0)
    ci = lax.broadcasted_iota(jnp.int32, (chunk, chunk), 1)
    for hh in range(heads):
        lo = hh * B_HD
        gvh = gv[:, lo:lo + B_HD]
        vn_h = gvh * lax.rsqrt(jnp.mean(gvh * gvh, axis=-1, keepdims=True) + EPS) * vg_ref[:, lo:lo + B_HD]
        if emit_vn:
            vn_ref[:, :, lo:lo + B_HD] = vn_h.reshape(bt, tt, B_HD)
        vn_hb = vn_h.astype(BF16)
        ws_h = jnp.where(ci <= ri, ws_ref[hh], 0.0).astype(BF16)
        bias = bst_ref[:, hh:hh + 1]
        for c in range(m // chunk):
            z = _dot(ws_h, vn_hb[c * chunk:(c + 1) * chunk, :]) + bias
            z_buf[c * chunk:(c + 1) * chunk, lo:lo + B_HD] = z
    yb = jax.nn.gelu(ub) * z_buf[...]
    yb_n = _rms(yb, gob_ref[...]).astype(BF16)

    y = _dot(ya_n, wout_ref[0:d_a, :]) + _dot(yb_n, wout_ref[d_a:, :])
    y_ref[...] = (x + y).reshape(bt, tt, d)


def _mixer(x, conv0, h0, p, *, bt, tt, pos0, emit_vn):
    nb, seq, d = x.shape
    d_a = conv0.shape[-1]
    d_b = p['vg'].shape[-1]
    chunk = min(tt, MLP_CHUNK)
    assert nb % bt == 0 and seq % tt == 0 and tt % chunk == 0 and tt % SUBLANES == 0
    grid = (nb // bt, seq // tt)
    weights = [p['gin'], p['win'], p['cw'], p['cb'], p['wr'], p['br'], p['wi'], p['bi'], p['lam'],
               p['vg'], p['ws'][:, :chunk, :chunk], p['bst'][:chunk], p['goa'], p['gob'], p['wout']]
    in_specs = [pl.BlockSpec((bt, tt, d), lambda b, t: (b, t, 0)),
                pl.BlockSpec((bt, CONV_W - 1, d_a), lambda b, t: (b, 0, 0)),
                pl.BlockSpec((bt, 1, d_a), lambda b, t: (b, 0, 0))]
    in_specs += [_full(w.shape) for w in weights]
    out_shape = [jax.ShapeDtypeStruct((nb, seq, d), F32),
                 jax.ShapeDtypeStruct((nb, CONV_W - 1, d_a), F32),
                 jax.ShapeDtypeStruct((nb, 1, d_a), F32)]
    out_specs = [pl.BlockSpec((bt, tt, d), lambda b, t: (b, t, 0)),
                 pl.BlockSpec((bt, CONV_W - 1, d_a), lambda b, t: (b, 0, 0)),
                 pl.BlockSpec((bt, 1, d_a), lambda b, t: (b, 0, 0))]
    if emit_vn:
        out_shape.append(jax.ShapeDtypeStruct((nb, seq, d_b), F32))
        out_specs.append(pl.BlockSpec((bt, tt, d_b), lambda b, t: (b, t, 0)))
    scratch = [pltpu.VMEM((bt, tt + 8, d_a), F32), pltpu.VMEM((bt, tt, d_a), F32),
               pltpu.VMEM((bt, tt, d_a), F32), pltpu.VMEM((bt, tt, d_a), F32),
               pltpu.VMEM((bt * tt, d_b), F32), pltpu.VMEM((bt, 1, d_a), F32)]
    kern = functools.partial(_mixer_kernel, bt=bt, tt=tt, chunk=chunk, pos0=pos0, emit_vn=emit_vn)
    return pl.pallas_call(
        kern, grid=grid, in_specs=in_specs, out_specs=out_specs, out_shape=out_shape,
        scratch_shapes=scratch, compiler_params=_cparams(("arbitrary", "arbitrary")),
        name="mixer")(x, conv0, h0, *weights)


def _memkv_kernel(m_ref, g_ref, wk_ref, wv_ref, k_ref, v_ref):
    mn = _rms(m_ref[...], g_ref[...]).astype(BF16)
    k_ref[...] = _dot(mn, wk_ref[...])
    v_ref[...] = _dot(mn, wv_ref[...])


def _mem_kv(mem2d, g, wk, wv, *, tm=512, tn=1024):
    n, d = mem2d.shape
    grid = (n // tm, d // tn)
    return pl.pallas_call(
        _memkv_kernel, grid=grid,
        in_specs=[pl.BlockSpec((tm, d), lambda i, j: (i, 0)), _full(g.shape),
                  pl.BlockSpec((d, tn), lambda i, j: (0, j)), pl.BlockSpec((d, tn), lambda i, j: (0, j))],
        out_specs=[pl.BlockSpec((tm, tn), lambda i, j: (i, j)), pl.BlockSpec((tm, tn), lambda i, j: (i, j))],
        out_shape=[jax.ShapeDtypeStruct((n, d), F32), jax.ShapeDtypeStruct((n, d), F32)],
        compiler_params=_cparams(("parallel", "parallel")), name="mem_kv")(mem2d, g, wk, wv)


def _attend_kernel(x_ref, k_ref, v_ref, g_ref, wq_ref, wo_ref, y_ref, o_buf, *, bt, tt):
    d = x_ref.shape[-1]
    hd = d // MEM_HEADS
    m = bt * tt
    x = x_ref[...].reshape(m, d)
    xn = _rms(x, g_ref[...]).astype(BF16)
    q = _dot(xn, wq_ref[...]).astype(BF16)
    scale = hd ** -0.5
    for b in range(bt):
        for hh in range(MEM_HEADS):
            qh = q[b * tt:(b + 1) * tt, hh * hd:(hh + 1) * hd]
            s = _dot_nt(qh, k_ref[b, :, hh * hd:(hh + 1) * hd]) * scale
            e = jnp.exp(s - jnp.max(s, axis=-1, keepdims=True))
            pr = e / jnp.sum(e, axis=-1, keepdims=True)
            o = _dot(pr.astype(BF16), v_ref[b, :, hh * hd:(hh + 1) * hd])
            o_buf[b * tt:(b + 1) * tt, hh * hd:(hh + 1) * hd] = o.astype(BF16)
    y_ref[...] = (x + _dot(o_buf[...], wo_ref[...])).reshape(bt, tt, d)


def _attend(x, k, v, g, wq, wo, *, bt, tt):
    nb, seq, d = x.shape
    mt = k.shape[1]
    grid = (nb // bt, seq // tt)
    kern = functools.partial(_attend_kernel, bt=bt, tt=tt)
    return pl.pallas_call(
        kern, grid=grid,
        in_specs=[pl.BlockSpec((bt, tt, d), lambda b, t: (b, t, 0)),
                  pl.BlockSpec((bt, mt, d), lambda b, t: (b, 0, 0)),
                  pl.BlockSpec((bt, mt, d), lambda b, t: (b, 0, 0)),
                  _full(g.shape), _full(wq.shape), _full(wo.shape)],
        out_specs=pl.BlockSpec((bt, tt, d), lambda b, t: (b, t, 0)),
        out_shape=jax.ShapeDtypeStruct((nb, seq, d), F32),
        scratch_shapes=[pltpu.VMEM((bt * tt, d), BF16)],
        compiler_params=_cparams(("parallel", "parallel")), name="attend")(x, k, v, g, wq, wo)


def _pack_halves(x):
    h = x.shape[-1] // 2
    hi = pltpu.bitcast(x[:, :h].astype(BF16).astype(F32), jnp.uint32)
    lo = pltpu.bitcast(x[:, h:].astype(BF16).astype(F32), jnp.uint32)
    return (hi & jnp.uint32(0xFFFF0000)) | (lo >> 16)


def _unpack_halves(p):
    hi = pltpu.bitcast(p & jnp.uint32(0xFFFF0000), F32)
    lo = pltpu.bitcast(p << 16, F32)
    return hi, lo


def _router_kernel(x_ref, g_ref, wth_ref, wtl_ref, bias_ref, xp_ref, eid_ref, gate_ref, rank_ref, cnt_ref):
    i = pl.program_id(0)
    tt = x_ref.shape[0]

    @pl.when(i == 0)
    def _():
        cnt_ref[...] = jnp.zeros_like(cnt_ref)

    xn = _rms(x_ref[...], g_ref[...])
    xp_ref[...] = _pack_halves(xn)
    xh = xn.astype(BF16)
    xl = (xn - xh.astype(F32)).astype(BF16)
    logits = (_dot_nt(wth_ref[...], xh) + _dot_nt(wth_ref[...], xl) + _dot_nt(wtl_ref[...], xh)) + bias_ref[...]
    lc = logits[0:N_GROUPS]
    ec = jnp.exp(lc - jnp.max(lc, axis=0, keepdims=True))
    pc = ec / jnp.sum(ec, axis=0, keepdims=True)
    p_grp = jnp.max(pc, axis=0, keepdims=True)
    gi = lax.broadcasted_iota(jnp.int32, pc.shape, 0)
    grp = jnp.min(jnp.where(pc == p_grp, gi, N_GROUPS), axis=0, keepdims=True)
    lf = jnp.zeros((EXPERTS_PER_GROUP, tt), F32)
    for gidx in range(N_GROUPS):
        lo = 8 + gidx * EXPERTS_PER_GROUP
        lf = lf + jnp.where(grp == gidx, logits[lo:lo + EXPERTS_PER_GROUP], 0.0)
    fi = lax.broadcasted_iota(jnp.int32, lf.shape, 0)
    t1 = jnp.max(lf, axis=0, keepdims=True)
    i1 = jnp.min(jnp.where(lf == t1, fi, EXPERTS_PER_GROUP), axis=0, keepdims=True)
    lf2 = jnp.where(fi == i1, -jnp.inf, lf)
    t2 = jnp.max(lf2, axis=0, keepdims=True)
    i2 = jnp.min(jnp.where(lf2 == t2, fi, EXPERTS_PER_GROUP), axis=0, keepdims=True)
    e2 = jnp.exp(t2 - t1)
    den = 1.0 + e2
    gate_ref[0:1, :] = p_grp * (1.0 / den)
    gate_ref[1:2, :] = p_grp * (e2 / den)
    eid1 = grp * EXPERTS_PER_GROUP + i1
    eid2 = grp * EXPERTS_PER_GROUP + i2
    eid_ref[0:1, :] = eid1
    eid_ref[1:2, :] = eid2

    ei = lax.broadcasted_iota(jnp.int32, (N_EXPERTS, tt), 0)
    oh1 = ei == eid1
    oh2 = ei == eid2
    oh = jnp.where(jnp.logical_or(oh1, oh2), 1.0, 0.0)
    tr = lax.broadcasted_iota(jnp.int32, (tt, tt), 0)
    tc = lax.broadcasted_iota(jnp.int32, (tt, tt), 1)
    upper = jnp.where(tr < tc, 1.0, 0.0).astype(BF16)
    run = cnt_ref[:, 0:1]
    before = _dot(oh.astype(BF16), upper) + run
    rank_ref[0:1, :] = jnp.sum(jnp.where(oh1, before, 0.0), axis=0, keepdims=True).astype(jnp.int32)
    rank_ref[1:2, :] = jnp.sum(jnp.where(oh2, before, 0.0), axis=0, keepdims=True).astype(jnp.int32)
    cnt_ref[...] = cnt_ref[...] + jnp.sum(oh, axis=1, keepdims=True)


def _router(x2d, g, wth, wtl, bias, *, tt):
    n, d = x2d.shape
    grid = (n // tt,)
    return pl.pallas_call(
        _router_kernel, grid=grid,
        in_specs=[pl.BlockSpec((tt, d), lambda i: (i, 0)), _full(g.shape), _full(wth.shape),
                  _full(wtl.shape), _full(bias.shape)],
        out_specs=[pl.BlockSpec((tt, d // 2), lambda i: (i, 0)),
                   pl.BlockSpec((2, tt), lambda i: (0, i)),
                   pl.BlockSpec((2, tt), lambda i: (0, i)),
                   pl.BlockSpec((2, tt), lambda i: (0, i)),
                   pl.BlockSpec((N_EXPERTS, 128), lambda i: (0, 0))],
        out_shape=[jax.ShapeDtypeStruct((n, d // 2), jnp.uint32),
                   jax.ShapeDtypeStruct((2, n), jnp.int32),
                   jax.ShapeDtypeStruct((2, n), F32),
                   jax.ShapeDtypeStruct((2, n), jnp.int32),
                   jax.ShapeDtypeStruct((N_EXPERTS, 128), F32)],
        compiler_params=_cparams(("arbitrary",)), name="router")(x2d, g, wth, wtl, bias)


def _dispatch_kernel(dest_ref, xp_ref, buf_in_ref, buf_ref, sem):
    del buf_in_ref
    tt = xp_ref.shape[0]

    def copy(r, k):
        return pltpu.make_async_copy(xp_ref.at[pl.ds(r, 1), :],
                                     buf_ref.at[pl.ds(dest_ref[k, r], 1), :], sem)

    def start(r, c):
        copy(r, 0).start()
        copy(r, 1).start()
        return c

    def wait(r, c):
        copy(r, 0).wait()
        copy(r, 1).wait()
        return c

    lax.fori_loop(0, tt, start, 0)
    lax.fori_loop(0, tt, wait, 0)


def _dispatch(dest, xp, buf0, *, tt):
    n, hw = xp.shape
    grid = (n // tt,)
    return pl.pallas_call(
        _dispatch_kernel, grid=grid,
        in_specs=[pl.BlockSpec((2, tt), lambda i: (0, i), memory_space=pltpu.SMEM),
                  pl.BlockSpec((tt, hw), lambda i: (i, 0)),
                  pl.BlockSpec(memory_space=pl.ANY)],
        out_specs=pl.BlockSpec(memory_space=pl.ANY),
        out_shape=jax.ShapeDtypeStruct(buf0.shape, buf0.dtype),
        scratch_shapes=[pltpu.SemaphoreType.DMA(())],
        input_output_aliases={2: 0},
        compiler_params=_cparams(("arbitrary",)), name="dispatch")(dest, xp, buf0)


def _experts_kernel(be_ref, nu_ref, x_ref, w1_ref, w3_ref, w2_ref, o_ref):
    i = pl.program_id(0)

    @pl.when(i < nu_ref[0])
    def _():
        hw = x_ref.shape[-1]
        hi, lo = _unpack_halves(x_ref[...])
        xa = hi.astype(BF16)
        xb = lo.astype(BF16)
        h1 = _dot(xa, w1_ref[0, 0:hw, :]) + _dot(xb, w1_ref[0, hw:, :])
        h3 = _dot(xa, w3_ref[0, 0:hw, :]) + _dot(xb, w3_ref[0, hw:, :])
        hd = (h1 * jax.nn.sigmoid(h1) * h3).astype(BF16)
        o_ref[...] = _pack_halves(_dot(hd, w2_ref[0]))

    @pl.when(i >= nu_ref[0])
    def _():
        o_ref[...] = jnp.zeros_like(o_ref)


def _experts(block_e, n_used, buf, w1, w3, w2, *, bm):
    p_rows, hw = buf.shape
    ne, d, de = w1.shape
    nblk = p_rows // bm
    gs = pltpu.PrefetchScalarGridSpec(
        num_scalar_prefetch=2, grid=(nblk,),
        in_specs=[pl.BlockSpec((bm, hw), lambda i, be, nu: (i, 0)),
                  pl.BlockSpec((1, d, de), lambda i, be, nu: (be[i], 0, 0)),
                  pl.BlockSpec((1, d, de), lambda i, be, nu: (be[i], 0, 0)),
                  pl.BlockSpec((1, de, d), lambda i, be, nu: (be[i], 0, 0))],
        out_specs=pl.BlockSpec((bm, hw), lambda i, be, nu: (i, 0)))
    return pl.pallas_call(
        _experts_kernel, grid_spec=gs,
        out_shape=jax.ShapeDtypeStruct((p_rows, hw), jnp.uint32),
        compiler_params=_cparams(("arbitrary",)), name="experts")(block_e, n_used, buf, w1, w3, w2)


def _combine_kernel(dest_ref, x_ref, gate_ref, g_ref, yb_ref, o_ref, r0_buf, r1_buf, sem):
    tt = x_ref.shape[0]
    bufs = (r0_buf, r1_buf)

    def copy(r, k):
        return pltpu.make_async_copy(yb_ref.at[pl.ds(dest_ref[k, r], 1), :],
                                     bufs[k].at[pl.ds(r, 1), :], sem)

    def start(r, c):
        copy(r, 0).start()
        copy(r, 1).start()
        return c

    def wait(r, c):
        copy(r, 0).wait()
        copy(r, 1).wait()
        return c

    lax.fori_loop(0, tt, start, 0)
    lax.fori_loop(0, tt, wait, 0)
    a_hi, a_lo = _unpack_halves(r0_buf[...])
    b_hi, b_lo = _unpack_halves(r1_buf[...])
    g0 = gate_ref[:, 0:1]
    g1 = gate_ref[:, 1:2]
    moe = jnp.concatenate([g0 * a_hi + g1 * b_hi, g0 * a_lo + g1 * b_lo], axis=-1)
    o_ref[...] = _rms(x_ref[...] + moe, g_ref[...])


def _combine(dest, x2d, gate_t, g, yb, *, tt):
    n, d = x2d.shape
    hw = yb.shape[-1]
    grid = (n // tt,)
    return pl.pallas_call(
        _combine_kernel, grid=grid,
        in_specs=[pl.BlockSpec((2, tt), lambda i: (0, i), memory_space=pltpu.SMEM),
                  pl.BlockSpec((tt, d), lambda i: (i, 0)),
                  pl.BlockSpec((tt, 2), lambda i: (i, 0)),
                  _full(g.shape),
                  pl.BlockSpec(memory_space=pl.ANY)],
        out_specs=pl.BlockSpec((tt, d), lambda i: (i, 0)),
        out_shape=jax.ShapeDtypeStruct((n, d), F32),
        scratch_shapes=[pltpu.VMEM((tt, hw), jnp.uint32), pltpu.VMEM((tt, hw), jnp.uint32),
                        pltpu.SemaphoreType.DMA(())],
        compiler_params=_cparams(("arbitrary",)), name="combine")(dest, x2d, gate_t, g, yb)


def _moe_final(x2, mp, g_final):
    nb, seq, d = x2.shape
    n = nb * seq
    x2d = x2.reshape(n, d)
    tt = min(ROUTE_ROWS, n)
    xp, eid, gate, rank, cnt = _router(x2d, mp['g'], mp['wth'], mp['wtl'], mp['bias'], tt=tt)
    bm = EXPERT_ROWS
    counts = cnt[:, 0].astype(jnp.int32)
    padded = (counts + bm - 1) // bm * bm
    pend = jnp.cumsum(padded)
    pstart = pend - padded
    dest = pstart[eid] + rank
    p_rows = (2 * n + N_EXPERTS * (bm - 1)) // bm * bm
    nblk = p_rows // bm
    block_start = jnp.arange(nblk, dtype=jnp.int32) * bm
    n_used = (pend[-1] // bm).astype(jnp.int32)
    block_e = jnp.sum((pend[None, :] <= block_start[:, None]).astype(jnp.int32), axis=1)
    last_e = jnp.sum((pend <= pend[-1] - 1).astype(jnp.int32))
    block_e = jnp.minimum(block_e, last_e).astype(jnp.int32)
    buf0 = jnp.zeros((p_rows, d // 2), jnp.uint32)
    buf = _dispatch(dest, xp, buf0, tt=tt)
    yb = _experts(block_e, n_used.reshape(1), buf, mp['w1'], mp['w3'], mp['w2'], bm=bm)
    out = _combine(dest, x2d, gate.T, g_final, yb, tt=tt)
    return out.reshape(nb, seq, d)


def _block_diag_pack(w, per):
    h, hd, _ = w.shape
    wg = w.reshape(h // per, per, hd, hd)
    eye = jnp.eye(per, dtype=w.dtype)
    return jnp.einsum('gpij,pq->gpiqj', wg, eye).reshape(h // per, per * hd, per * hd)


def kernel(x_prompt, x_sample, mem_prompt, state_conv_a, state_rglru_h, cache_mem_k, cache_mem_v, norm_mix, w_in, conv_a_w, conv_a_b, lru_w_r, lru_b_r, lru_w_i, lru_b_i, lru_lambda, v_norm_g, mlp_w_s, mlp_b_s, out_norm_a, out_norm_b, w_out, norm_mem_q, norm_mem_kv, w_mem_q, w_mem_k, w_mem_v, w_mem_o, norm_ffn, w_router_c, b_router_c, w_router_f, b_router_f, w_exp_1, w_exp_3, w_exp_2, norm_final):
    depth = w_in.shape[0]
    assert depth == 1
    l = 0
    bp, tp, d = x_prompt.shape
    bs, ts, _ = x_sample.shape
    d_a = conv_a_b.shape[-1]
    a_hd = lru_w_r.shape[-1]
    per = V7X_MXU_DIM // a_hd
    row = lambda v: v.reshape(1, -1)

    mix_p = dict(
        gin=row(norm_mix[l]), win=w_in[l].astype(BF16), cw=conv_a_w[l], cb=row(conv_a_b[l]),
        wr=_block_diag_pack(lru_w_r[l], per).astype(BF16), br=row(lru_b_r[l]),
        wi=_block_diag_pack(lru_w_i[l], per).astype(BF16), bi=row(lru_b_i[l]),
        lam=row(lru_lambda[l]), vg=row(v_norm_g[l]), ws=mlp_w_s[l], bst=mlp_b_s[l].T,
        goa=row(out_norm_a[l]), gob=row(out_norm_b[l]), wout=w_out[l].astype(BF16))

    conv0 = jnp.zeros((bp, CONV_W - 1, d_a), F32)
    h0 = jnp.zeros((bp, 1, d_a), F32)
    tt_p = min(MIX_ROWS, tp)
    x1p, conv_p, h_p = _mixer(x_prompt, conv0, h0, mix_p, bt=1, tt=tt_p, pos0=0, emit_vn=False)
    bt_mix = 8 if bs % 8 == 0 else 1
    x1s, conv_s, h_s, vn_s = _mixer(x_sample, state_conv_a[l], state_rglru_h[l][:, None, :], mix_p,
                                    bt=bt_mix, tt=ts, pos0=PAST_LEN, emit_vn=True)

    mt = mem_prompt.shape[1]
    k_p, v_p = _mem_kv(mem_prompt.reshape(bp * mt, d), row(norm_mem_kv[l]),
                       w_mem_k[l].astype(BF16), w_mem_v[l].astype(BF16))
    k_p = k_p.reshape(bp, mt, d)
    v_p = v_p.reshape(bp, mt, d)
    gq = row(norm_mem_q[l])
    wq = w_mem_q[l].astype(BF16)
    wo = w_mem_o[l].astype(BF16)
    x2p = _attend(x1p, k_p.astype(BF16), v_p.astype(BF16), gq, wq, wo, bt=1, tt=tt_p)
    bt_s = 4 if bs % 4 == 0 else 1
    x2s = _attend(x1s, cache_mem_k[l].reshape(bs, mt, d).astype(BF16),
                  cache_mem_v[l].reshape(bs, mt, d).astype(BF16), gq, wq, wo, bt=bt_s, tt=ts)

    wt = jnp.concatenate([w_router_c[l].T, jnp.zeros((8 - N_GROUPS, d), F32), w_router_f[l].T], axis=0)
    wth = wt.astype(BF16)
    wtl = (wt - wth.astype(F32)).astype(BF16)
    bias = jnp.concatenate([b_router_c[l], jnp.zeros((8 - N_GROUPS,), F32), b_router_f[l]]).reshape(-1, 1)
    moe_p = dict(g=row(norm_ffn[l]), wth=wth, wtl=wtl, bias=bias,
                 w1=w_exp_1[l].astype(BF16), w3=w_exp_3[l].astype(BF16), w2=w_exp_2[l].astype(BF16))
    gf = row(norm_final)
    y_prompt = _moe_final(x2p, moe_p, gf)
    y_sample = _moe_final(x2s, moe_p, gf)

    heads_b = v_norm_g.shape[1]
    hd_m = d // MEM_HEADS
    return (y_prompt, y_sample,
            conv_p[None], h_p.reshape(1, bp, d_a),
            k_p.reshape(1, bp, mt, MEM_HEADS, hd_m), v_p.reshape(1, bp, mt, MEM_HEADS, hd_m),
            conv_s[None], h_s.reshape(1, bs, d_a),
            vn_s.reshape(1, bs, ts, heads_b, B_HD))
```

```python
import functools

import jax
import jax.numpy as jnp
from jax import lax
from jax.experimental import pallas as pl
from jax.experimental.pallas import tpu as pltpu

EPS = 1e-6
LRU_C = 8.0
CONV_W = 4
A_HEADS = 16
B_HD = 128
MLP_CHUNK = 128
MEM_HEADS = 4
N_GROUPS = 4
EXPERTS_PER_GROUP = 8
N_EXPERTS = N_GROUPS * EXPERTS_PER_GROUP
PAST_LEN = 1024

BF16 = jnp.bfloat16
F32 = jnp.float32

V7X_VMEM_BYTES = 64 * 1024 * 1024
V7X_MXU_DIM = 256
SUBLANES = 8

MIX_ROWS = 256
ROUTE_ROWS = 512
EXPERT_ROWS = 256
VMEM_LIMIT = 56 * 1024 * 1024


def _cparams(sem):
    return pltpu.CompilerParams(dimension_semantics=sem, vmem_limit_bytes=VMEM_LIMIT)


def _rms(x, g):
    return x * lax.rsqrt(jnp.mean(x * x, axis=-1, keepdims=True) + EPS) * g


def _dot(a, b):
    return jnp.dot(a, b, preferred_element_type=F32)


def _dot_nt(a, b):
    return lax.dot_general(a, b, (((1,), (1,)), ((), ())), preferred_element_type=F32)


def _full(shape):
    nd = len(shape)
    return pl.BlockSpec(shape, lambda *_: (0,) * nd)


def _mixer_kernel(x_ref, conv0_ref, h0_ref, gin_ref, win_ref, cw_ref, cb_ref, wr_ref, br_ref,
                  wi_ref, bi_ref, lam_ref, vg_ref, ws_ref, bst_ref, goa_ref, gob_ref, wout_ref,
                  *rest, bt, tt, chunk, pos0, emit_vn):
    if emit_vn:
        y_ref, convo_ref, ho_ref, vn_ref = rest[:4]
        scratch = rest[4:]
    else:
        y_ref, convo_ref, ho_ref = rest[:3]
        vn_ref = None
        scratch = rest[3:]
    xa_buf, a_buf, u_buf, h_buf, z_buf, hc_ref = scratch
    t = pl.program_id(1)
    d = x_ref.shape[-1]
    d_a = a_buf.shape[-1]
    d_b = z_buf.shape[-1]
    m = bt * tt

    @pl.when(t == 0)
    def _():
        xa_buf[:, 5:8, :] = conv0_ref[...]
        hc_ref[...] = h0_ref[...]

    x = x_ref[...].reshape(m, d)
    xn = _rms(x, gin_ref[...]).astype(BF16)
    xa = _dot(xn, win_ref[:, 0:d_a])
    ga = _dot(xn, win_ref[:, d_a:2 * d_a])
    ub = _dot(xn, win_ref[:, 2 * d_a:2 * d_a + d_b])
    vb = _dot(xn, win_ref[:, 2 * d_a + d_b:])

    xa_buf[:, 8:8 + tt, :] = xa.reshape(bt, tt, d_a)
    xc = cb_ref[...][None]
    for k in range(CONV_W):
        xc = xc + xa_buf[:, 5 + k:5 + k + tt, :] * cw_ref[k:k + 1, :][None]
    tail = xa_buf[:, tt + 5:tt + 8, :]
    convo_ref[...] = tail
    xa_buf[:, 5:8, :] = tail
    xc = xc.reshape(m, d_a)

    xcb = xc.astype(BF16)
    gw = V7X_MXU_DIM
    r_pre = jnp.concatenate([_dot(xcb[:, g * gw:(g + 1) * gw], wr_ref[g]) for g in range(d_a // gw)], axis=-1)
    i_pre = jnp.concatenate([_dot(xcb[:, g * gw:(g + 1) * gw], wi_ref[g]) for g in range(d_a // gw)], axis=-1)
    r = jax.nn.sigmoid(r_pre + br_ref[...])
    ig = jax.nn.sigmoid(i_pre + bi_ref[...])
    nl = -lam_ref[...]
    softplus = jnp.maximum(nl, 0.0) + jnp.log1p(jnp.exp(-jnp.abs(nl)))
    log_a = -LRU_C * r * softplus
    a = jnp.exp(log_a)
    mult = jnp.sqrt(-jnp.tanh(log_a) * (1.0 + a * a))
    if pos0 == 0:
        row = lax.broadcasted_iota(jnp.int32, (m, d_a), 0)
        is_first = jnp.logical_and(t == 0, (row % tt) == 0)
        mult = jnp.where(is_first, 1.0, mult)
    u = mult * (ig * xc)
    a_buf[...] = a.reshape(bt, tt, d_a)
    u_buf[...] = u.reshape(bt, tt, d_a)

    sub = lax.broadcasted_iota(jnp.int32, (SUBLANES, d_a), 0)
    for b in range(bt):
        h = hc_ref[b]
        for j in range(tt // SUBLANES):
            r0 = j * SUBLANES
            av = a_buf[b, r0:r0 + SUBLANES, :]
            uv = u_buf[b, r0:r0 + SUBLANES, :]
            for s in (1, 2, 4):
                a_sh = jnp.where(sub >= s, pltpu.roll(av, s, 0), 1.0)
                u_sh = jnp.where(sub >= s, pltpu.roll(uv, s, 0), 0.0)
                uv = av * u_sh + uv
                av = av * a_sh
            hv = av * h + uv
            h_buf[b, r0:r0 + SUBLANES, :] = hv
            h = hv[SUBLANES - 1:SUBLANES, :]
        hc_ref[b] = h
    ho_ref[...] = hc_ref[...]

    ya = h_buf[...].reshape(m, d_a) * jax.nn.gelu(ga)
    ya_n = _rms(ya, goa_ref[...]).astype(BF16)

    gv = jax.nn.gelu(vb)
    heads = d_b // B_HD
    ri = lax.broadcasted_iota(jnp.int32, (chunk, chunk), 0)
    ci = lax.broadcasted_iota(jnp.int32, (chunk, chunk), 1)
    for hh in range(heads):
        lo = hh * B_HD
        gvh = gv[:, lo:lo + B_HD]
        vn_h = gvh * lax.rsqrt(jnp.mean(gvh * gvh, axis=-1, keepdims=True) + EPS) * vg_ref[:, lo:lo + B_HD]
        if emit_vn:
            vn_ref[:, :, lo:lo + B_HD] = vn_h.reshape(bt, tt, B_HD)
        vn_hb = vn_h.astype(BF16)
        ws_h = jnp.where(ci <= ri, ws_ref[hh], 0.0).astype(BF16)
        bias = bst_ref[:, hh:hh + 1]
        for c in range(m // chunk):
            z = _dot(ws_h, vn_hb[c * chunk:(c + 1) * chunk, :]) + bias
            z_buf[c * chunk:(c + 1) * chunk, lo:lo + B_HD] = z
    yb = jax.nn.gelu(ub) * z_buf[...]
    yb_n = _rms(yb, gob_ref[...]).astype(BF16)

    y = _dot(ya_n, wout_ref[0:d_a, :]) + _dot(yb_n, wout_ref[d_a:, :])
    y_ref[...] = (x + y).reshape(bt, tt, d)


def _mixer(x, conv0, h0, p, *, bt, tt, pos0, emit_vn):
    nb, seq, d = x.shape
    d_a = conv0.shape[-1]
    d_b = p['vg'].shape[-1]
    chunk = min(tt, MLP_CHUNK)
    assert nb % bt == 0 and seq % tt == 0 and tt % chunk == 0 and tt % SUBLANES == 0
    grid = (nb // bt, seq // tt)
    weights = [p['gin'], p['win'], p['cw'], p['cb'], p['wr'], p['br'], p['wi'], p['bi'], p['lam'],
               p['vg'], p['ws'][:, :chunk, :chunk], p['bst'][:chunk], p['goa'], p['gob'], p['wout']]
    in_specs = [pl.BlockSpec((bt, tt, d), lambda b, t: (b, t, 0)),
                pl.BlockSpec((bt, CONV_W - 1, d_a), lambda b, t: (b, 0, 0)),
                pl.BlockSpec((bt, 1, d_a), lambda b, t: (b, 0, 0))]
    in_specs += [_full(w.shape) for w in weights]
    out_shape = [jax.ShapeDtypeStruct((nb, seq, d), F32),
                 jax.ShapeDtypeStruct((nb, CONV_W - 1, d_a), F32),
                 jax.ShapeDtypeStruct((nb, 1, d_a), F32)]
    out_specs = [pl.BlockSpec((bt, tt, d), lambda b, t: (b, t, 0)),
                 pl.BlockSpec((bt, CONV_W - 1, d_a), lambda b, t: (b, 0, 0)),
                 pl.BlockSpec((bt, 1, d_a), lambda b, t: (b, 0, 0))]
    if emit_vn:
        out_shape.append(jax.ShapeDtypeStruct((nb, seq, d_b), F32))
        out_specs.append(pl.BlockSpec((bt, tt, d_b), lambda b, t: (b, t, 0)))
    scratch = [pltpu.VMEM((bt, tt + 8, d_a), F32), pltpu.VMEM((bt, tt, d_a), F32),
               pltpu.VMEM((bt, tt, d_a), F32), pltpu.VMEM((bt, tt, d_a), F32),
               pltpu.VMEM((bt * tt, d_b), F32), pltpu.VMEM((bt, 1, d_a), F32)]
    kern = functools.partial(_mixer_kernel, bt=bt, tt=tt, chunk=chunk, pos0=pos0, emit_vn=emit_vn)
    return pl.pallas_call(
        kern, grid=grid, in_specs=in_specs, out_specs=out_specs, out_shape=out_shape,
        scratch_shapes=scratch, compiler_params=_cparams(("arbitrary", "arbitrary")),
        name="mixer")(x, conv0, h0, *weights)


def _memkv_kernel(m_ref, g_ref, wk_ref, wv_ref, k_ref, v_ref, kb_ref, vb_ref):
    mn = _rms(m_ref[...], g_ref[...]).astype(BF16)
    k = _dot(mn, wk_ref[...])
    v = _dot(mn, wv_ref[...])
    k_ref[...] = k
    v_ref[...] = v
    kb_ref[...] = k.astype(BF16)
    vb_ref[...] = v.astype(BF16)


def _mem_kv(mem2d, g, wk, wv, *, tm=512, tn=1024):
    n, d = mem2d.shape
    grid = (n // tm, d // tn)
    o_spec = pl.BlockSpec((tm, tn), lambda i, j: (i, j))
    return pl.pallas_call(
        _memkv_kernel, grid=grid,
        in_specs=[pl.BlockSpec((tm, d), lambda i, j: (i, 0)), _full(g.shape),
                  pl.BlockSpec((d, tn), lambda i, j: (0, j)), pl.BlockSpec((d, tn), lambda i, j: (0, j))],
        out_specs=[o_spec, o_spec, o_spec, o_spec],
        out_shape=[jax.ShapeDtypeStruct((n, d), F32), jax.ShapeDtypeStruct((n, d), F32),
                   jax.ShapeDtypeStruct((n, d), BF16), jax.ShapeDtypeStruct((n, d), BF16)],
        compiler_params=_cparams(("parallel", "parallel")), name="mem_kv")(mem2d, g, wk, wv)


def _attend_kernel(x_ref, k_ref, v_ref, g_ref, wq_ref, wo_ref, y_ref, o_buf, *, bt, tt):
    d = x_ref.shape[-1]
    hd = d // MEM_HEADS
    x = x_ref[...]
    xn = _rms(x, g_ref[...]).astype(BF16)
    q = _dot(xn, wq_ref[...]).astype(BF16)
    scale = hd ** -0.5
    for b in range(bt):
        for hh in range(MEM_HEADS):
            qh = q[b * tt:(b + 1) * tt, hh * hd:(hh + 1) * hd]
            s = _dot_nt(qh, k_ref[b, :, hh * hd:(hh + 1) * hd]) * scale
            e = jnp.exp(s - jnp.max(s, axis=-1, keepdims=True))
            pr = e / jnp.sum(e, axis=-1, keepdims=True)
            o = _dot(pr.astype(BF16), v_ref[b, :, hh * hd:(hh + 1) * hd])
            o_buf[b * tt:(b + 1) * tt, hh * hd:(hh + 1) * hd] = o.astype(BF16)
    y_ref[...] = x + _dot(o_buf[...], wo_ref[...])


def _attend(x, k, v, g, wq, wo, *, bt, tt):
    nb, seq, d = x.shape
    mt = k.shape[1]
    m = bt * tt
    n = nb * seq
    assert n % m == 0 and (bt == 1 or tt == seq)
    kern = functools.partial(_attend_kernel, bt=bt, tt=tt)
    kv_spec = pl.BlockSpec((bt, mt, d), lambda i: ((i * m) // (bt * seq), 0, 0))
    return pl.pallas_call(
        kern, grid=(n // m,),
        in_specs=[pl.BlockSpec((m, d), lambda i: (i, 0)), kv_spec, kv_spec,
                  _full(g.shape), _full(wq.shape), _full(wo.shape)],
        out_specs=pl.BlockSpec((m, d), lambda i: (i, 0)),
        out_shape=jax.ShapeDtypeStruct((n, d), F32),
        scratch_shapes=[pltpu.VMEM((m, d), BF16)],
        compiler_params=_cparams(("arbitrary",)), name="attend")(x.reshape(n, d), k, v, g, wq, wo)


def _two_source_specs(tt, d, npb):
    return [pl.BlockSpec((tt, d), lambda i, *_: (jnp.minimum(i, npb - 1), 0)),
            pl.BlockSpec((tt, d), lambda i, *_: (jnp.maximum(i - npb, 0), 0))]


def _router_kernel(xp_ref, xs_ref, g_ref, wth_ref, wtl_ref, bias_ref, eid_ref, gate_ref, rank_ref, cnt_ref, *, npb):
    i = pl.program_id(0)
    tt = xp_ref.shape[0]

    @pl.when(i == 0)
    def _():
        cnt_ref[...] = jnp.zeros_like(cnt_ref)

    x = jnp.where(i < npb, xp_ref[...], xs_ref[...])
    xn = _rms(x, g_ref[...])
    xh = xn.astype(BF16)
    xl = (xn - xh.astype(F32)).astype(BF16)
    logits = (_dot_nt(wth_ref[...], xh) + _dot_nt(wth_ref[...], xl) + _dot_nt(wtl_ref[...], xh)) + bias_ref[...]
    lc = logits[0:N_GROUPS]
    ec = jnp.exp(lc - jnp.max(lc, axis=0, keepdims=True))
    pc = ec / jnp.sum(ec, axis=0, keepdims=True)
    p_grp = jnp.max(pc, axis=0, keepdims=True)
    gi = lax.broadcasted_iota(jnp.int32, pc.shape, 0)
    grp = jnp.min(jnp.where(pc == p_grp, gi, N_GROUPS), axis=0, keepdims=True)
    lf = jnp.zeros((EXPERTS_PER_GROUP, tt), F32)
    for gidx in range(N_GROUPS):
        lo = 8 + gidx * EXPERTS_PER_GROUP
        lf = lf + jnp.where(grp == gidx, logits[lo:lo + EXPERTS_PER_GROUP], 0.0)
    fi = lax.broadcasted_iota(jnp.int32, lf.shape, 0)
    t1 = jnp.max(lf, axis=0, keepdims=True)
    i1 = jnp.min(jnp.where(lf == t1, fi, EXPERTS_PER_GROUP), axis=0, keepdims=True)
    lf2 = jnp.where(fi == i1, -jnp.inf, lf)
    t2 = jnp.max(lf2, axis=0, keepdims=True)
    i2 = jnp.min(jnp.where(lf2 == t2, fi, EXPERTS_PER_GROUP), axis=0, keepdims=True)
    e2 = jnp.exp(t2 - t1)
    den = 1.0 + e2
    gate_ref[0:1, :] = p_grp * (1.0 / den)
    gate_ref[1:2, :] = p_grp * (e2 / den)
    eid1 = grp * EXPERTS_PER_GROUP + i1
    eid2 = grp * EXPERTS_PER_GROUP + i2
    eid_ref[0:1, :] = eid1
    eid_ref[1:2, :] = eid2

    ei = lax.broadcasted_iota(jnp.int32, (N_EXPERTS, tt), 0)
    oh1 = ei == eid1
    oh2 = ei == eid2
    oh = jnp.where(jnp.logical_or(oh1, oh2), 1.0, 0.0)
    tr = lax.broadcasted_iota(jnp.int32, (tt, tt), 0)
    tc = lax.broadcasted_iota(jnp.int32, (tt, tt), 1)
    upper = jnp.where(tr < tc, 1.0, 0.0).astype(BF16)
    run = cnt_ref[:, 0:1]
    before = _dot(oh.astype(BF16), upper) + run
    rank_ref[0:1, :] = jnp.sum(jnp.where(oh1, before, 0.0), axis=0, keepdims=True).astype(jnp.int32)
    rank_ref[1:2, :] = jnp.sum(jnp.where(oh2, before, 0.0), axis=0, keepdims=True).astype(jnp.int32)
    cnt_ref[...] = cnt_ref[...] + jnp.sum(oh, axis=1, keepdims=True)


def _router(xp, xs, g, wth, wtl, bias, *, tt):
    (n_p, d), n_s = xp.shape, xs.shape[0]
    npb, n = n_p // tt, n_p + n_s
    tok = pl.BlockSpec((2, tt), lambda i: (0, i))
    return pl.pallas_call(
        functools.partial(_router_kernel, npb=npb), grid=(n // tt,),
        in_specs=_two_source_specs(tt, d, npb) + [_full(g.shape), _full(wth.shape), _full(wtl.shape),
                                                  _full(bias.shape)],
        out_specs=[tok, tok, tok, pl.BlockSpec((N_EXPERTS, 128), lambda i: (0, 0))],
        out_shape=[jax.ShapeDtypeStruct((2, n), jnp.int32),
                   jax.ShapeDtypeStruct((2, n), F32),
                   jax.ShapeDtypeStruct((2, n), jnp.int32),
                   jax.ShapeDtypeStruct((N_EXPERTS, 128), F32)],
        compiler_params=_cparams(("arbitrary",)), name="router")(xp, xs, g, wth, wtl, bias)


def _dispatch_kernel(zs_ref, has_ref, nu_ref, dest_ref, xp_ref, xs_ref, buf_ref, zero_buf, sem, zsem, *, npb, bm):
    i = pl.program_id(0)
    tt = xp_ref.shape[0]
    nblk = buf_ref.shape[0] // bm

    @pl.when(i == 0)
    def _():
        zero_buf[...] = jnp.zeros_like(zero_buf)

        def zcopy(row0):
            return pltpu.make_async_copy(zero_buf, buf_ref.at[pl.ds(pl.multiple_of(row0, bm), bm), :], zsem)

        for e in range(N_EXPERTS):
            @pl.when(has_ref[e] > 0)
            def _():
                zcopy(zs_ref[e]).start()

        def tail_start(blk, c):
            zcopy(blk * bm).start()
            return c

        def tail_wait(blk, c):
            zcopy(blk * bm).wait()
            return c

        lax.fori_loop(nu_ref[0], nblk, tail_start, 0)
        for e in range(N_EXPERTS):
            @pl.when(has_ref[e] > 0)
            def _():
                zcopy(zs_ref[e]).wait()
        lax.fori_loop(nu_ref[0], nblk, tail_wait, 0)

    def scatter(x_ref):
        def group(gidx, c):
            r0 = pl.multiple_of(gidx * SUBLANES, SUBLANES)
            for j in range(SUBLANES):
                for k in range(2):
                    pltpu.make_async_copy(x_ref.at[pl.ds(r0 + j, 1), :],
                                          buf_ref.at[pl.ds(dest_ref[k, r0 + j], 1), :], sem).start()
            return c

        lax.fori_loop(0, tt // SUBLANES, group, 0)
        for k in range(2):
            pltpu.make_async_copy(x_ref, buf_ref.at[pl.ds(0, tt), :], sem).wait()

    @pl.when(i < npb)
    def _():
        scatter(xp_ref)

    @pl.when(i >= npb)
    def _():
        scatter(xs_ref)


def _dispatch(zstart, has, n_used, dest, xp, xs, p_rows, *, tt, bm):
    (n_p, d), n_s = xp.shape, xs.shape[0]
    npb, n = n_p // tt, n_p + n_s
    gs = pltpu.PrefetchScalarGridSpec(
        num_scalar_prefetch=3, grid=(n // tt,),
        in_specs=[pl.BlockSpec((2, tt), lambda i, *_: (0, i), memory_space=pltpu.SMEM)]
        + _two_source_specs(tt, d, npb),
        out_specs=pl.BlockSpec(memory_space=pl.ANY),
        scratch_shapes=[pltpu.VMEM((bm, d), F32), pltpu.SemaphoreType.DMA(()), pltpu.SemaphoreType.DMA(())])
    return pl.pallas_call(
        functools.partial(_dispatch_kernel, npb=npb, bm=bm), grid_spec=gs,
        out_shape=jax.ShapeDtypeStruct((p_rows, d), F32),
        compiler_params=_cparams(("arbitrary",)), name="dispatch")(zstart, has, n_used, dest, xp, xs)


def _experts_kernel(be_ref, nu_ref, x_ref, g_ref, w1_ref, w3_ref, w2_ref, o_ref, w1b, w3b, w2b):
    i = pl.program_id(0)

    @pl.when(i < nu_ref[0])
    def _():
        prev = be_ref[jnp.maximum(i - 1, 0)]

        @pl.when(jnp.logical_or(i == 0, be_ref[i] != prev))
        def _():
            w1b[...] = w1_ref[0].astype(BF16)
            w3b[...] = w3_ref[0].astype(BF16)
            w2b[...] = w2_ref[0].astype(BF16)

        xn = _rms(x_ref[...], g_ref[...]).astype(BF16)
        h1 = _dot(xn, w1b[...])
        h3 = _dot(xn, w3b[...])
        hd = (h1 * jax.nn.sigmoid(h1) * h3).astype(BF16)
        o_ref[...] = _dot(hd, w2b[...])

    @pl.when(i >= nu_ref[0])
    def _():
        o_ref[...] = jnp.zeros_like(o_ref)


def _experts(block_e, n_used, buf, g, w1, w3, w2, *, bm):
    p_rows, d = buf.shape
    ne, _, de = w1.shape
    nblk = p_rows // bm
    gs = pltpu.PrefetchScalarGridSpec(
        num_scalar_prefetch=2, grid=(nblk,),
        in_specs=[pl.BlockSpec((bm, d), lambda i, be, nu: (jnp.minimum(i, nu[0] - 1), 0)),
                  pl.BlockSpec(g.shape, lambda i, be, nu: (0, 0)),
                  pl.BlockSpec((1, d, de), lambda i, be, nu: (be[i], 0, 0)),
                  pl.BlockSpec((1, d, de), lambda i, be, nu: (be[i], 0, 0)),
                  pl.BlockSpec((1, de, d), lambda i, be, nu: (be[i], 0, 0))],
        out_specs=pl.BlockSpec((bm, d), lambda i, be, nu: (i, 0)),
        scratch_shapes=[pltpu.VMEM((d, de), BF16), pltpu.VMEM((d, de), BF16), pltpu.VMEM((de, d), BF16)])
    return pl.pallas_call(
        _experts_kernel, grid_spec=gs,
        out_shape=jax.ShapeDtypeStruct((p_rows, d), F32),
        compiler_params=_cparams(("arbitrary",)), name="experts")(block_e, n_used, buf, g, w1, w3, w2)


def _combine_kernel(dest_ref, x_ref, gate_ref, g_ref, yb_ref, o_ref, r0_buf, r1_buf, sem):
    tt = x_ref.shape[0]
    bufs = (r0_buf, r1_buf)

    def group(gidx, c):
        r0 = pl.multiple_of(gidx * SUBLANES, SUBLANES)
        for j in range(SUBLANES):
            for k in range(2):
                pltpu.make_async_copy(yb_ref.at[pl.ds(dest_ref[k, r0 + j], 1), :],
                                      bufs[k].at[pl.ds(r0 + j, 1), :], sem).start()
        return c

    lax.fori_loop(0, tt // SUBLANES, group, 0)
    for k in range(2):
        pltpu.make_async_copy(yb_ref.at[pl.ds(0, tt), :], bufs[k], sem).wait()
    moe = gate_ref[:, 0:1] * r0_buf[...] + gate_ref[:, 1:2] * r1_buf[...]
    o_ref[...] = _rms(x_ref[...] + moe, g_ref[...])


def _combine(dest, x2d, gate_t, g, yb, *, tt, row_off):
    n, d = x2d.shape
    off = row_off // tt
    return pl.pallas_call(
        _combine_kernel, grid=(n // tt,),
        in_specs=[pl.BlockSpec((2, tt), lambda i: (0, i + off), memory_space=pltpu.SMEM),
                  pl.BlockSpec((tt, d), lambda i: (i, 0)),
                  pl.BlockSpec((tt, 2), lambda i: (i + off, 0)),
                  _full(g.shape),
                  pl.BlockSpec(memory_space=pl.ANY)],
        out_specs=pl.BlockSpec((tt, d), lambda i: (i, 0)),
        out_shape=jax.ShapeDtypeStruct((n, d), F32),
        scratch_shapes=[pltpu.VMEM((tt, d), F32), pltpu.VMEM((tt, d), F32), pltpu.SemaphoreType.DMA(())],
        compiler_params=_cparams(("arbitrary",)), name="combine")(dest, x2d, gate_t, g, yb)


def _moe_final(xp, xs, mp, g_final):
    (n_p, d), n_s = xp.shape, xs.shape[0]
    n = n_p + n_s
    tt = min(ROUTE_ROWS, n_s)
    assert n_p % tt == 0 and n_s % tt == 0
    eid, gate, rank, cnt = _router(xp, xs, mp['g'], mp['wth'], mp['wtl'], mp['bias'], tt=tt)
    bm = EXPERT_ROWS
    counts = cnt[:, 0].astype(jnp.int32)
    padded = (counts + bm - 1) // bm * bm
    pend = jnp.cumsum(padded)
    pstart = pend - padded
    sel = eid[:, :, None] == jnp.arange(N_EXPERTS, dtype=jnp.int32)
    dest = rank + jnp.sum(jnp.where(sel, pstart, 0), axis=-1)
    p_rows = (2 * n + N_EXPERTS * (bm - 1)) // bm * bm
    nblk = p_rows // bm
    block_start = jnp.arange(nblk, dtype=jnp.int32) * bm
    n_used = (pend[-1] // bm).astype(jnp.int32)
    block_e = jnp.sum((pend[None, :] <= block_start[:, None]).astype(jnp.int32), axis=1)
    last_e = jnp.sum((pend <= pend[-1] - 1).astype(jnp.int32))
    block_e = jnp.minimum(block_e, last_e).astype(jnp.int32)
    n_used = n_used.reshape(1)
    buf = _dispatch((pend - bm).astype(jnp.int32), counts, n_used, dest, xp, xs, p_rows, tt=tt, bm=bm)
    yb = _experts(block_e, n_used, buf, mp['g'], mp['w1'], mp['w3'], mp['w2'], bm=bm)
    gate_t = gate.T
    y_p = _combine(dest, xp, gate_t, g_final, yb, tt=tt, row_off=0)
    y_s = _combine(dest, xs, gate_t, g_final, yb, tt=tt, row_off=n_p)
    return y_p, y_s


def _block_diag_pack(w, per):
    h, hd, _ = w.shape
    wg = w.reshape(h // per, per, hd, hd)
    eye = jnp.eye(per, dtype=w.dtype)
    return jnp.einsum('gpij,pq->gpiqj', wg, eye).reshape(h // per, per * hd, per * hd)


def kernel(x_prompt, x_sample, mem_prompt, state_conv_a, state_rglru_h, cache_mem_k, cache_mem_v, norm_mix, w_in, conv_a_w, conv_a_b, lru_w_r, lru_b_r, lru_w_i, lru_b_i, lru_lambda, v_norm_g, mlp_w_s, mlp_b_s, out_norm_a, out_norm_b, w_out, norm_mem_q, norm_mem_kv, w_mem_q, w_mem_k, w_mem_v, w_mem_o, norm_ffn, w_router_c, b_router_c, w_router_f, b_router_f, w_exp_1, w_exp_3, w_exp_2, norm_final):
    depth = w_in.shape[0]
    assert depth == 1
    l = 0
    bp, tp, d = x_prompt.shape
    bs, ts, _ = x_sample.shape
    d_a = conv_a_b.shape[-1]
    a_hd = lru_w_r.shape[-1]
    per = V7X_MXU_DIM // a_hd
    row = lambda v: v.reshape(1, -1)

    mix_p = dict(
        gin=row(norm_mix[l]), win=w_in[l].astype(BF16), cw=conv_a_w[l], cb=row(conv_a_b[l]),
        wr=_block_diag_pack(lru_w_r[l], per).astype(BF16), br=row(lru_b_r[l]),
        wi=_block_diag_pack(lru_w_i[l], per).astype(BF16), bi=row(lru_b_i[l]),
        lam=row(lru_lambda[l]), vg=row(v_norm_g[l]), ws=mlp_w_s[l], bst=mlp_b_s[l].T,
        goa=row(out_norm_a[l]), gob=row(out_norm_b[l]), wout=w_out[l].astype(BF16))

    conv0 = jnp.zeros((bp, CONV_W - 1, d_a), F32)
    h0 = jnp.zeros((bp, 1, d_a), F32)
    tt_p = min(MIX_ROWS, tp)
    x1p, conv_p, h_p = _mixer(x_prompt, conv0, h0, mix_p, bt=1, tt=tt_p, pos0=0, emit_vn=False)
    bt_mix = 8 if bs % 8 == 0 else 1
    x1s, conv_s, h_s, vn_s = _mixer(x_sample, state_conv_a[l], state_rglru_h[l][:, None, :], mix_p,
                                    bt=bt_mix, tt=ts, pos0=PAST_LEN, emit_vn=True)

    mt = mem_prompt.shape[1]
    k_p, v_p, kb_p, vb_p = _mem_kv(mem_prompt.reshape(bp * mt, d), row(norm_mem_kv[l]),
                                   w_mem_k[l].astype(BF16), w_mem_v[l].astype(BF16))
    gq = row(norm_mem_q[l])
    wq = w_mem_q[l].astype(BF16)
    wo = w_mem_o[l].astype(BF16)
    x2p = _attend(x1p, kb_p.reshape(bp, mt, d), vb_p.reshape(bp, mt, d), gq, wq, wo, bt=1, tt=tt_p)
    bt_s = 4 if bs % 4 == 0 else 1
    x2s = _attend(x1s, cache_mem_k[l].reshape(bs, mt, d).astype(BF16),
                  cache_mem_v[l].reshape(bs, mt, d).astype(BF16), gq, wq, wo, bt=bt_s, tt=ts)

    wt = jnp.concatenate([w_router_c[l].T, jnp.zeros((8 - N_GROUPS, d), F32), w_router_f[l].T], axis=0)
    wth = wt.astype(BF16)
    wtl = (wt - wth.astype(F32)).astype(BF16)
    bias = jnp.concatenate([b_router_c[l], jnp.zeros((8 - N_GROUPS,), F32), b_router_f[l]]).reshape(-1, 1)
    moe_p = dict(g=row(norm_ffn[l]), wth=wth, wtl=wtl, bias=bias, w1=w_exp_1[l], w3=w_exp_3[l], w2=w_exp_2[l])
    y_p, y_s = _moe_final(x2p, x2s, moe_p, row(norm_final))

    heads_b = v_norm_g.shape[1]
    hd_m = d // MEM_HEADS
    return (y_p.reshape(bp, tp, d), y_s.reshape(bs, ts, d),
            conv_p[None], h_p.reshape(1, bp, d_a),
            k_p.reshape(1, bp, mt, MEM_HEADS, hd_m), v_p.reshape(1, bp, mt, MEM_HEADS, hd_m),
            conv_s[None], h_s.reshape(1, bs, d_a),
            vn_s.reshape(1, bs, ts, heads_b, B_HD))
```

```python
import functools

import jax
import jax.numpy as jnp
from jax import lax
from jax.experimental import pallas as pl
from jax.experimental.pallas import tpu as pltpu

EPS = 1e-6
LRU_C = 8.0
CONV_W = 4
A_HEADS = 16
B_HD = 128
MLP_CHUNK = 128
MEM_HEADS = 4
N_GROUPS = 4
EXPERTS_PER_GROUP = 8
N_EXPERTS = N_GROUPS * EXPERTS_PER_GROUP
PAST_LEN = 1024

BF16 = jnp.bfloat16
F32 = jnp.float32

V7X_VMEM_BYTES = 64 * 1024 * 1024
V7X_MXU_DIM = 256
SUBLANES = 8

MIX_ROWS = 256
ROUTE_ROWS = 512
EXPERT_ROWS = 256
VMEM_LIMIT = 56 * 1024 * 1024


def _cparams(sem):
    return pltpu.CompilerParams(dimension_semantics=sem, vmem_limit_bytes=VMEM_LIMIT)


def _rms(x, g):
    return x * lax.rsqrt(jnp.mean(x * x, axis=-1, keepdims=True) + EPS) * g


def _dot(a, b):
    return jnp.dot(a, b, preferred_element_type=F32)


def _dot_nt(a, b):
    return lax.dot_general(a, b, (((1,), (1,)), ((), ())), preferred_element_type=F32)


def _full(shape):
    nd = len(shape)
    return pl.BlockSpec(shape, lambda *_: (0,) * nd)


def _mixer_kernel(x_ref, conv0_ref, h0_ref, gin_ref, win_ref, cw_ref, cb_ref, wr_ref, br_ref,
                  wi_ref, bi_ref, lam_ref, vg_ref, ws_ref, bst_ref, goa_ref, gob_ref, wout_ref,
                  *rest, bt, tt, chunk, pos0, emit_vn):
    if emit_vn:
        y_ref, convo_ref, ho_ref, vn_ref = rest[:4]
        scratch = rest[4:]
    else:
        y_ref, convo_ref, ho_ref = rest[:3]
        vn_ref = None
        scratch = rest[3:]
    xa_buf, a_buf, u_buf, h_buf, z_buf, hc_ref, ycat = scratch
    t = pl.program_id(1)
    d = x_ref.shape[-1]
    d_a = a_buf.shape[-1]
    d_b = z_buf.shape[-1]
    m = bt * tt

    @pl.when(t == 0)
    def _():
        xa_buf[:, 5:8, :] = conv0_ref[...]
        hc_ref[...] = h0_ref[...]

    x = x_ref[...].reshape(m, d)
    xn = _rms(x, gin_ref[...]).astype(BF16)
    xa = _dot(xn, win_ref[:, 0:d_a])
    ga = _dot(xn, win_ref[:, d_a:2 * d_a])
    ub = _dot(xn, win_ref[:, 2 * d_a:2 * d_a + d_b])
    vb = _dot(xn, win_ref[:, 2 * d_a + d_b:])

    xa_buf[:, 8:8 + tt, :] = xa.reshape(bt, tt, d_a)
    xc = cb_ref[...][None]
    for k in range(CONV_W):
        xc = xc + xa_buf[:, 5 + k:5 + k + tt, :] * cw_ref[k:k + 1, :][None]
    tail = xa_buf[:, tt + 5:tt + 8, :]
    convo_ref[...] = tail
    xa_buf[:, 5:8, :] = tail
    xc = xc.reshape(m, d_a)

    xcb = xc.astype(BF16)
    gw = V7X_MXU_DIM
    r_pre = jnp.concatenate([_dot(xcb[:, g * gw:(g + 1) * gw], wr_ref[g]) for g in range(d_a // gw)], axis=-1)
    i_pre = jnp.concatenate([_dot(xcb[:, g * gw:(g + 1) * gw], wi_ref[g]) for g in range(d_a // gw)], axis=-1)
    r = jax.nn.sigmoid(r_pre + br_ref[...])
    ig = jax.nn.sigmoid(i_pre + bi_ref[...])
    nl = -lam_ref[...]
    softplus = jnp.maximum(nl, 0.0) + jnp.log1p(jnp.exp(-jnp.abs(nl)))
    log_a = -LRU_C * r * softplus
    a = jnp.exp(log_a)
    m2 = -jnp.tanh(log_a) * (1.0 + a * a)
    mult = jnp.where(m2 > 0.0, m2 * lax.rsqrt(m2), 0.0)
    if pos0 == 0:
        row = lax.broadcasted_iota(jnp.int32, (m, d_a), 0)
        is_first = jnp.logical_and(t == 0, (row % tt) == 0)
        mult = jnp.where(is_first, 1.0, mult)
    u = mult * (ig * xc)
    a_buf[...] = a.reshape(bt, tt, d_a)
    u_buf[...] = u.reshape(bt, tt, d_a)

    sub = lax.broadcasted_iota(jnp.int32, (SUBLANES, d_a), 0)
    for b in range(bt):
        h = hc_ref[b]
        for j in range(tt // SUBLANES):
            r0 = j * SUBLANES
            av = a_buf[b, r0:r0 + SUBLANES, :]
            uv = u_buf[b, r0:r0 + SUBLANES, :]
            for s in (1, 2, 4):
                a_sh = jnp.where(sub >= s, pltpu.roll(av, s, 0), 1.0)
                u_sh = jnp.where(sub >= s, pltpu.roll(uv, s, 0), 0.0)
                uv = av * u_sh + uv
                av = av * a_sh
            hv = av * h + uv
            h_buf[b, r0:r0 + SUBLANES, :] = hv
            h = hv[SUBLANES - 1:SUBLANES, :]
        hc_ref[b] = h
    ho_ref[...] = hc_ref[...]

    ya = h_buf[...].reshape(m, d_a) * jax.nn.gelu(ga)
    ycat[:, 0:d_a] = _rms(ya, goa_ref[...]).astype(BF16)

    gv = jax.nn.gelu(vb)
    heads = d_b // B_HD
    ri = lax.broadcasted_iota(jnp.int32, (chunk, chunk), 0)
    ci = lax.broadcasted_iota(jnp.int32, (chunk, chunk), 1)
    for hh in range(heads):
        lo = hh * B_HD
        gvh = gv[:, lo:lo + B_HD]
        vn_h = gvh * lax.rsqrt(jnp.mean(gvh * gvh, axis=-1, keepdims=True) + EPS) * vg_ref[:, lo:lo + B_HD]
        if emit_vn:
            vn_ref[:, :, lo:lo + B_HD] = vn_h.reshape(bt, tt, B_HD)
        vn_hb = vn_h.astype(BF16)
        ws_h = jnp.where(ci <= ri, ws_ref[hh], 0.0).astype(BF16)
        bias = bst_ref[:, hh:hh + 1]
        for c in range(m // chunk):
            z = _dot(ws_h, vn_hb[c * chunk:(c + 1) * chunk, :]) + bias
            z_buf[c * chunk:(c + 1) * chunk, lo:lo + B_HD] = z
    yb = jax.nn.gelu(ub) * z_buf[...]
    ycat[:, d_a:] = _rms(yb, gob_ref[...]).astype(BF16)

    y_ref[...] = (x + _dot(ycat[...], wout_ref[...])).reshape(bt, tt, d)


def _mixer(x, conv0, h0, p, *, bt, tt, pos0, emit_vn):
    nb, seq, d = x.shape
    d_a = conv0.shape[-1]
    d_b = p['vg'].shape[-1]
    chunk = min(tt, MLP_CHUNK)
    assert nb % bt == 0 and seq % tt == 0 and tt % chunk == 0 and tt % SUBLANES == 0
    grid = (nb // bt, seq // tt)
    weights = [p['gin'], p['win'], p['cw'], p['cb'], p['wr'], p['br'], p['wi'], p['bi'], p['lam'],
               p['vg'], p['ws'][:, :chunk, :chunk], p['bst'][:chunk], p['goa'], p['gob'], p['wout']]
    in_specs = [pl.BlockSpec((bt, tt, d), lambda b, t: (b, t, 0)),
                pl.BlockSpec((bt, CONV_W - 1, d_a), lambda b, t: (b, 0, 0)),
                pl.BlockSpec((bt, 1, d_a), lambda b, t: (b, 0, 0))]
    in_specs += [_full(w.shape) for w in weights]
    out_shape = [jax.ShapeDtypeStruct((nb, seq, d), F32),
                 jax.ShapeDtypeStruct((nb, CONV_W - 1, d_a), F32),
                 jax.ShapeDtypeStruct((nb, 1, d_a), F32)]
    out_specs = [pl.BlockSpec((bt, tt, d), lambda b, t: (b, t, 0)),
                 pl.BlockSpec((bt, CONV_W - 1, d_a), lambda b, t: (b, 0, 0)),
                 pl.BlockSpec((bt, 1, d_a), lambda b, t: (b, 0, 0))]
    if emit_vn:
        out_shape.append(jax.ShapeDtypeStruct((nb, seq, d_b), F32))
        out_specs.append(pl.BlockSpec((bt, tt, d_b), lambda b, t: (b, t, 0)))
    scratch = [pltpu.VMEM((bt, tt + 8, d_a), F32), pltpu.VMEM((bt, tt, d_a), F32),
               pltpu.VMEM((bt, tt, d_a), F32), pltpu.VMEM((bt, tt, d_a), F32),
               pltpu.VMEM((bt * tt, d_b), F32), pltpu.VMEM((bt, 1, d_a), F32),
               pltpu.VMEM((bt * tt, d), BF16)]
    kern = functools.partial(_mixer_kernel, bt=bt, tt=tt, chunk=chunk, pos0=pos0, emit_vn=emit_vn)
    return pl.pallas_call(
        kern, grid=grid, in_specs=in_specs, out_specs=out_specs, out_shape=out_shape,
        scratch_shapes=scratch, compiler_params=_cparams(("arbitrary", "arbitrary")),
        name="mixer")(x, conv0, h0, *weights)


def _memkv_kernel(m_ref, g_ref, wk_ref, wv_ref, k_ref, v_ref, kb_ref, vb_ref):
    mn = _rms(m_ref[...], g_ref[...]).astype(BF16)
    k = _dot(mn, wk_ref[...])
    v = _dot(mn, wv_ref[...])
    k_ref[...] = k
    v_ref[...] = v
    kb_ref[...] = k.astype(BF16)
    vb_ref[...] = v.astype(BF16)


def _mem_kv(mem2d, g, wk, wv, *, tm=512, tn=1024):
    n, d = mem2d.shape
    grid = (n // tm, d // tn)
    o_spec = pl.BlockSpec((tm, tn), lambda i, j: (i, j))
    return pl.pallas_call(
        _memkv_kernel, grid=grid,
        in_specs=[pl.BlockSpec((tm, d), lambda i, j: (i, 0)), _full(g.shape),
                  pl.BlockSpec((d, tn), lambda i, j: (0, j)), pl.BlockSpec((d, tn), lambda i, j: (0, j))],
        out_specs=[o_spec, o_spec, o_spec, o_spec],
        out_shape=[jax.ShapeDtypeStruct((n, d), F32), jax.ShapeDtypeStruct((n, d), F32),
                   jax.ShapeDtypeStruct((n, d), BF16), jax.ShapeDtypeStruct((n, d), BF16)],
        compiler_params=_cparams(("parallel", "parallel")), name="mem_kv")(mem2d, g, wk, wv)


def _attend_kernel(x_ref, k_ref, v_ref, g_ref, wq_ref, wo_ref, y_ref, o_buf, *, bt, tt):
    d = x_ref.shape[-1]
    hd = d // MEM_HEADS
    x = x_ref[...]
    xn = _rms(x, g_ref[...]).astype(BF16)
    q = _dot(xn, wq_ref[...]).astype(BF16)
    scale = hd ** -0.5
    for b in range(bt):
        for hh in range(MEM_HEADS):
            qh = q[b * tt:(b + 1) * tt, hh * hd:(hh + 1) * hd]
            s = _dot_nt(qh, k_ref[b, :, hh * hd:(hh + 1) * hd]) * scale
            e = jnp.exp(s - jnp.max(s, axis=-1, keepdims=True))
            pr = e / jnp.sum(e, axis=-1, keepdims=True)
            o = _dot(pr.astype(BF16), v_ref[b, :, hh * hd:(hh + 1) * hd])
            o_buf[b * tt:(b + 1) * tt, hh * hd:(hh + 1) * hd] = o.astype(BF16)
    y_ref[...] = x + _dot(o_buf[...], wo_ref[...])


def _attend(x, k, v, g, wq, wo, *, bt, tt):
    nb, seq, d = x.shape
    mt = k.shape[1]
    m = bt * tt
    n = nb * seq
    assert n % m == 0 and (bt == 1 or tt == seq)
    kern = functools.partial(_attend_kernel, bt=bt, tt=tt)
    kv_spec = pl.BlockSpec((bt, mt, d), lambda i: ((i * m) // (bt * seq), 0, 0))
    return pl.pallas_call(
        kern, grid=(n // m,),
        in_specs=[pl.BlockSpec((m, d), lambda i: (i, 0)), kv_spec, kv_spec,
                  _full(g.shape), _full(wq.shape), _full(wo.shape)],
        out_specs=pl.BlockSpec((m, d), lambda i: (i, 0)),
        out_shape=jax.ShapeDtypeStruct((n, d), F32),
        scratch_shapes=[pltpu.VMEM((m, d), BF16)],
        compiler_params=_cparams(("arbitrary",)), name="attend")(x.reshape(n, d), k, v, g, wq, wo)


def _two_source_specs(tt, d, npb):
    return [pl.BlockSpec((tt, d), lambda i, *_: (jnp.minimum(i, npb - 1), 0)),
            pl.BlockSpec((tt, d), lambda i, *_: (jnp.maximum(i - npb, 0), 0))]


def _router_kernel(xp_ref, xs_ref, g_ref, wth_ref, wtl_ref, bias_ref, eid_ref, gate_ref, rank_ref, cnt_ref, *, npb):
    i = pl.program_id(0)
    tt = xp_ref.shape[0]

    @pl.when(i == 0)
    def _():
        cnt_ref[...] = jnp.zeros_like(cnt_ref)

    x = jnp.where(i < npb, xp_ref[...], xs_ref[...])
    xn = _rms(x, g_ref[...])
    xh = xn.astype(BF16)
    xl = (xn - xh.astype(F32)).astype(BF16)
    logits = (_dot_nt(wth_ref[...], xh) + _dot_nt(wth_ref[...], xl) + _dot_nt(wtl_ref[...], xh)) + bias_ref[...]
    lc = logits[0:N_GROUPS]
    ec = jnp.exp(lc - jnp.max(lc, axis=0, keepdims=True))
    pc = ec / jnp.sum(ec, axis=0, keepdims=True)
    p_grp = jnp.max(pc, axis=0, keepdims=True)
    gi = lax.broadcasted_iota(jnp.int32, pc.shape, 0)
    grp = jnp.min(jnp.where(pc == p_grp, gi, N_GROUPS), axis=0, keepdims=True)
    lf = jnp.zeros((EXPERTS_PER_GROUP, tt), F32)
    for gidx in range(N_GROUPS):
        lo = 8 + gidx * EXPERTS_PER_GROUP
        lf = lf + jnp.where(grp == gidx, logits[lo:lo + EXPERTS_PER_GROUP], 0.0)
    fi = lax.broadcasted_iota(jnp.int32, lf.shape, 0)
    t1 = jnp.max(lf, axis=0, keepdims=True)
    i1 = jnp.min(jnp.where(lf == t1, fi, EXPERTS_PER_GROUP), axis=0, keepdims=True)
    lf2 = jnp.where(fi == i1, -jnp.inf, lf)
    t2 = jnp.max(lf2, axis=0, keepdims=True)
    i2 = jnp.min(jnp.where(lf2 == t2, fi, EXPERTS_PER_GROUP), axis=0, keepdims=True)
    e2 = jnp.exp(t2 - t1)
    den = 1.0 + e2
    gate_ref[0:1, :] = p_grp * (1.0 / den)
    gate_ref[1:2, :] = p_grp * (e2 / den)
    eid1 = grp * EXPERTS_PER_GROUP + i1
    eid2 = grp * EXPERTS_PER_GROUP + i2
    eid_ref[0:1, :] = eid1
    eid_ref[1:2, :] = eid2

    ei = lax.broadcasted_iota(jnp.int32, (N_EXPERTS, tt), 0)
    oh1 = ei == eid1
    oh2 = ei == eid2
    oh = jnp.where(jnp.logical_or(oh1, oh2), 1.0, 0.0)
    tr = lax.broadcasted_iota(jnp.int32, (tt, tt), 0)
    tc = lax.broadcasted_iota(jnp.int32, (tt, tt), 1)
    upper = jnp.where(tr < tc, 1.0, 0.0).astype(BF16)
    run = cnt_ref[:, 0:1]
    before = _dot(oh.astype(BF16), upper) + run
    rank_ref[0:1, :] = jnp.sum(jnp.where(oh1, before, 0.0), axis=0, keepdims=True).astype(jnp.int32)
    rank_ref[1:2, :] = jnp.sum(jnp.where(oh2, before, 0.0), axis=0, keepdims=True).astype(jnp.int32)
    cnt_ref[...] = cnt_ref[...] + jnp.sum(oh, axis=1, keepdims=True)


def _router(xp, xs, g, wth, wtl, bias, *, tt):
    (n_p, d), n_s = xp.shape, xs.shape[0]
    npb, n = n_p // tt, n_p + n_s
    tok = pl.BlockSpec((2, tt), lambda i: (0, i))
    return pl.pallas_call(
        functools.partial(_router_kernel, npb=npb), grid=(n // tt,),
        in_specs=_two_source_specs(tt, d, npb) + [_full(g.shape), _full(wth.shape), _full(wtl.shape),
                                                  _full(bias.shape)],
        out_specs=[tok, tok, tok, pl.BlockSpec((N_EXPERTS, 128), lambda i: (0, 0))],
        out_shape=[jax.ShapeDtypeStruct((2, n), jnp.int32),
                   jax.ShapeDtypeStruct((2, n), F32),
                   jax.ShapeDtypeStruct((2, n), jnp.int32),
                   jax.ShapeDtypeStruct((N_EXPERTS, 128), F32)],
        compiler_params=_cparams(("arbitrary",)), name="router")(xp, xs, g, wth, wtl, bias)


def _dispatch_kernel(zs_ref, has_ref, nu_ref, dest_ref, xp_ref, xs_ref, buf_ref, zero_buf, sem, zsem, *, npb, bm):
    i = pl.program_id(0)
    tt = xp_ref.shape[0]
    nblk = buf_ref.shape[0] // bm

    @pl.when(i == 0)
    def _():
        zero_buf[...] = jnp.zeros_like(zero_buf)

        def zcopy(row0):
            return pltpu.make_async_copy(zero_buf, buf_ref.at[pl.ds(pl.multiple_of(row0, bm), bm), :], zsem)

        for e in range(N_EXPERTS):
            @pl.when(has_ref[e] > 0)
            def _():
                zcopy(zs_ref[e]).start()

        def tail_start(blk, c):
            zcopy(blk * bm).start()
            return c

        def tail_wait(blk, c):
            zcopy(blk * bm).wait()
            return c

        lax.fori_loop(nu_ref[0], nblk, tail_start, 0)
        for e in range(N_EXPERTS):
            @pl.when(has_ref[e] > 0)
            def _():
                zcopy(zs_ref[e]).wait()
        lax.fori_loop(nu_ref[0], nblk, tail_wait, 0)

    def scatter(x_ref):
        def group(gidx, c):
            r0 = pl.multiple_of(gidx * SUBLANES, SUBLANES)
            rows = x_ref.at[pl.ds(r0, SUBLANES), :]
            for j in range(SUBLANES):
                for k in range(2):
                    pltpu.make_async_copy(rows.at[pl.ds(j, 1), :],
                                          buf_ref.at[pl.ds(dest_ref[2 * r0 + (2 * j + k)], 1), :], sem).start()
            return c

        lax.fori_loop(0, tt // SUBLANES, group, 0)
        for k in range(2):
            pltpu.make_async_copy(x_ref, buf_ref.at[pl.ds(0, tt), :], sem).wait()

    @pl.when(i < npb)
    def _():
        scatter(xp_ref)

    @pl.when(i >= npb)
    def _():
        scatter(xs_ref)


def _dispatch(zstart, has, n_used, dest, xp, xs, p_rows, *, tt, bm):
    (n_p, d), n_s = xp.shape, xs.shape[0]
    npb, n = n_p // tt, n_p + n_s
    gs = pltpu.PrefetchScalarGridSpec(
        num_scalar_prefetch=3, grid=(n // tt,),
        in_specs=[pl.BlockSpec((2 * tt,), lambda i, *_: (i,), memory_space=pltpu.SMEM)]
        + _two_source_specs(tt, d, npb),
        out_specs=pl.BlockSpec(memory_space=pl.ANY),
        scratch_shapes=[pltpu.VMEM((bm, d), F32), pltpu.SemaphoreType.DMA(()), pltpu.SemaphoreType.DMA(())])
    return pl.pallas_call(
        functools.partial(_dispatch_kernel, npb=npb, bm=bm), grid_spec=gs,
        out_shape=jax.ShapeDtypeStruct((p_rows, d), F32),
        compiler_params=_cparams(("arbitrary",)), name="dispatch")(zstart, has, n_used, dest, xp, xs)


def _experts_kernel(be_ref, nu_ref, x_ref, g_ref, w1_ref, w3_ref, w2_ref, o_ref, w1b, w3b, w2b):
    i = pl.program_id(0)

    @pl.when(i < nu_ref[0])
    def _():
        prev = be_ref[jnp.maximum(i - 1, 0)]

        @pl.when(jnp.logical_or(i == 0, be_ref[i] != prev))
        def _():
            w1b[...] = w1_ref[0].astype(BF16)
            w3b[...] = w3_ref[0].astype(BF16)
            w2b[...] = w2_ref[0].astype(BF16)

        x = x_ref[...]
        xg = (x * g_ref[...]).astype(BF16)
        scale = lax.rsqrt(jnp.mean(x * x, axis=-1, keepdims=True) + EPS)
        h1 = _dot(xg, w1b[...]) * scale
        h3 = _dot(xg, w3b[...]) * scale
        hd = (h1 * jax.nn.sigmoid(h1) * h3).astype(BF16)
        o_ref[...] = _dot(hd, w2b[...])

    @pl.when(i >= nu_ref[0])
    def _():
        o_ref[...] = jnp.zeros_like(o_ref)


def _experts(block_e, n_used, buf, g, w1, w3, w2, *, bm):
    p_rows, d = buf.shape
    ne, _, de = w1.shape
    nblk = p_rows // bm
    gs = pltpu.PrefetchScalarGridSpec(
        num_scalar_prefetch=2, grid=(nblk,),
        in_specs=[pl.BlockSpec((bm, d), lambda i, be, nu: (jnp.minimum(i, nu[0] - 1), 0)),
                  pl.BlockSpec(g.shape, lambda i, be, nu: (0, 0)),
                  pl.BlockSpec((1, d, de), lambda i, be, nu: (be[i], 0, 0)),
                  pl.BlockSpec((1, d, de), lambda i, be, nu: (be[i], 0, 0)),
                  pl.BlockSpec((1, de, d), lambda i, be, nu: (be[i], 0, 0))],
        out_specs=pl.BlockSpec((bm, d), lambda i, be, nu: (i, 0)),
        scratch_shapes=[pltpu.VMEM((d, de), BF16), pltpu.VMEM((d, de), BF16), pltpu.VMEM((de, d), BF16)])
    return pl.pallas_call(
        _experts_kernel, grid_spec=gs,
        out_shape=jax.ShapeDtypeStruct((p_rows, d), F32),
        compiler_params=_cparams(("arbitrary",)), name="experts")(block_e, n_used, buf, g, w1, w3, w2)


def _combine_kernel(dest_ref, x_ref, gate_ref, g_ref, yb_ref, o_ref, a0, b0, a1, b1, sems, *, last):
    i = pl.program_id(0)
    tt = x_ref.shape[0]
    slots = ((a0, b0), (a1, b1))

    def issue(slot):
        for r in range(tt):
            for k in range(2):
                pltpu.make_async_copy(yb_ref.at[pl.ds(dest_ref[2 * r + k], 1), :],
                                      slots[slot][k].at[pl.ds(r, 1), :], sems.at[slot]).start()

    def wait(slot):
        for k in range(2):
            pltpu.make_async_copy(yb_ref.at[pl.ds(0, tt), :], slots[slot][k], sems.at[slot]).wait()

    @pl.when(i == 0)
    def _():
        issue(0)

    for par in range(2):
        @pl.when(jnp.logical_and(i >= 1, i % 2 == par))
        def _(par=par):
            cur = 1 - par
            wait(cur)
            issue(par)
            moe = gate_ref[:, 0:1] * slots[cur][0][...] + gate_ref[:, 1:2] * slots[cur][1][...]
            o_ref[...] = _rms(x_ref[...] + moe, g_ref[...])

            if par == last % 2:
                @pl.when(i == last)
                def _():
                    wait(par)


def _combine(dest_flat, x2d, gate_t, g, yb, *, tt, row_off):
    n, d = x2d.shape
    off = row_off // tt
    nt = n // tt
    prev = lambda i: jnp.maximum(i - 1, 0)
    return pl.pallas_call(
        functools.partial(_combine_kernel, last=nt), grid=(nt + 1,),
        in_specs=[pl.BlockSpec((2 * tt,), lambda i: (jnp.minimum(i, nt - 1) + off,), memory_space=pltpu.SMEM),
                  pl.BlockSpec((tt, d), lambda i: (prev(i), 0)),
                  pl.BlockSpec((tt, 2), lambda i: (prev(i) + off, 0)),
                  _full(g.shape),
                  pl.BlockSpec(memory_space=pl.ANY)],
        out_specs=pl.BlockSpec((tt, d), lambda i: (prev(i), 0)),
        out_shape=jax.ShapeDtypeStruct((n, d), F32),
        scratch_shapes=[pltpu.VMEM((tt, d), F32)] * 4 + [pltpu.SemaphoreType.DMA((2,))],
        compiler_params=_cparams(("arbitrary",)), name="combine")(dest_flat, x2d, gate_t, g, yb)


def _moe_final(xp, xs, mp, g_final):
    (n_p, d), n_s = xp.shape, xs.shape[0]
    n = n_p + n_s
    tt = min(ROUTE_ROWS, n_s)
    assert n_p % tt == 0 and n_s % tt == 0
    eid, gate, rank, cnt = _router(xp, xs, mp['g'], mp['wth'], mp['wtl'], mp['bias'], tt=tt)
    bm = EXPERT_ROWS
    counts = cnt[:, 0].astype(jnp.int32)
    padded = (counts + bm - 1) // bm * bm
    pend = jnp.cumsum(padded)
    pstart = pend - padded
    sel = eid[:, :, None] == jnp.arange(N_EXPERTS, dtype=jnp.int32)
    dest = rank + jnp.sum(jnp.where(sel, pstart, 0), axis=-1)
    dest = dest.T.reshape(-1)
    p_rows = (2 * n + N_EXPERTS * (bm - 1)) // bm * bm
    nblk = p_rows // bm
    block_start = jnp.arange(nblk, dtype=jnp.int32) * bm
    n_used = (pend[-1] // bm).astype(jnp.int32)
    block_e = jnp.sum((pend[None, :] <= block_start[:, None]).astype(jnp.int32), axis=1)
    last_e = jnp.sum((pend <= pend[-1] - 1).astype(jnp.int32))
    block_e = jnp.minimum(block_e, last_e).astype(jnp.int32)
    n_used = n_used.reshape(1)
    buf = _dispatch((pend - bm).astype(jnp.int32), counts, n_used, dest, xp, xs, p_rows, tt=tt, bm=bm)
    yb = _experts(block_e, n_used, buf, mp['g'], mp['w1'], mp['w3'], mp['w2'], bm=bm)
    gate_t = gate.T
    y_p = _combine(dest, xp, gate_t, g_final, yb, tt=tt, row_off=0)
    y_s = _combine(dest, xs, gate_t, g_final, yb, tt=tt, row_off=n_p)
    return y_p, y_s


def _block_diag_pack(w, per):
    h, hd, _ = w.shape
    wg = w.reshape(h // per, per, hd, hd)
    eye = jnp.eye(per, dtype=w.dtype)
    return jnp.einsum('gpij,pq->gpiqj', wg, eye).reshape(h // per, per * hd, per * hd)


def kernel(x_prompt, x_sample, mem_prompt, state_conv_a, state_rglru_h, cache_mem_k, cache_mem_v, norm_mix, w_in, conv_a_w, conv_a_b, lru_w_r, lru_b_r, lru_w_i, lru_b_i, lru_lambda, v_norm_g, mlp_w_s, mlp_b_s, out_norm_a, out_norm_b, w_out, norm_mem_q, norm_mem_kv, w_mem_q, w_mem_k, w_mem_v, w_mem_o, norm_ffn, w_router_c, b_router_c, w_router_f, b_router_f, w_exp_1, w_exp_3, w_exp_2, norm_final):
    depth = w_in.shape[0]
    assert depth == 1
    l = 0
    bp, tp, d = x_prompt.shape
    bs, ts, _ = x_sample.shape
    d_a = conv_a_b.shape[-1]
    a_hd = lru_w_r.shape[-1]
    per = V7X_MXU_DIM // a_hd
    row = lambda v: v.reshape(1, -1)

    mix_p = dict(
        gin=row(norm_mix[l]), win=w_in[l].astype(BF16), cw=conv_a_w[l], cb=row(conv_a_b[l]),
        wr=_block_diag_pack(lru_w_r[l], per).astype(BF16), br=row(lru_b_r[l]),
        wi=_block_diag_pack(lru_w_i[l], per).astype(BF16), bi=row(lru_b_i[l]),
        lam=row(lru_lambda[l]), vg=row(v_norm_g[l]), ws=mlp_w_s[l], bst=mlp_b_s[l].T,
        goa=row(out_norm_a[l]), gob=row(out_norm_b[l]), wout=w_out[l].astype(BF16))

    conv0 = jnp.zeros((bp, CONV_W - 1, d_a), F32)
    h0 = jnp.zeros((bp, 1, d_a), F32)
    tt_p = min(MIX_ROWS, tp)
    x1p, conv_p, h_p = _mixer(x_prompt, conv0, h0, mix_p, bt=1, tt=tt_p, pos0=0, emit_vn=False)
    bt_mix = 8 if bs % 8 == 0 else 1
    x1s, conv_s, h_s, vn_s = _mixer(x_sample, state_conv_a[l], state_rglru_h[l][:, None, :], mix_p,
                                    bt=bt_mix, tt=ts, pos0=PAST_LEN, emit_vn=True)

    mt = mem_prompt.shape[1]
    k_p, v_p, kb_p, vb_p = _mem_kv(mem_prompt.reshape(bp * mt, d), row(norm_mem_kv[l]),
                                   w_mem_k[l].astype(BF16), w_mem_v[l].astype(BF16))
    gq = row(norm_mem_q[l])
    wq = w_mem_q[l].astype(BF16)
    wo = w_mem_o[l].astype(BF16)
    x2p = _attend(x1p, kb_p.reshape(bp, mt, d), vb_p.reshape(bp, mt, d), gq, wq, wo, bt=1, tt=tt_p)
    bt_s = 4 if bs % 4 == 0 else 1
    x2s = _attend(x1s, cache_mem_k[l].reshape(bs, mt, d).astype(BF16),
                  cache_mem_v[l].reshape(bs, mt, d).astype(BF16), gq, wq, wo, bt=bt_s, tt=ts)

    wt = jnp.concatenate([w_router_c[l].T, jnp.zeros((8 - N_GROUPS, d), F32), w_router_f[l].T], axis=0)
    wth = wt.astype(BF16)
    wtl = (wt - wth.astype(F32)).astype(BF16)
    bias = jnp.concatenate([b_router_c[l], jnp.zeros((8 - N_GROUPS,), F32), b_router_f[l]]).reshape(-1, 1)
    moe_p = dict(g=row(norm_ffn[l]), wth=wth, wtl=wtl, bias=bias, w1=w_exp_1[l], w3=w_exp_3[l], w2=w_exp_2[l])
    y_p, y_s = _moe_final(x2p, x2s, moe_p, row(norm_final))

    heads_b = v_norm_g.shape[1]
    hd_m = d // MEM_HEADS
    return (y_p.reshape(bp, tp, d), y_s.reshape(bs, ts, d),
            conv_p[None], h_p.reshape(1, bp, d_a),
            k_p.reshape(1, bp, mt, MEM_HEADS, hd_m), v_p.reshape(1, bp, mt, MEM_HEADS, hd_m),
            conv_s[None], h_s.reshape(1, bs, d_a),
            vn_s.reshape(1, bs, ts, heads_b, B_HD))
```

```python
import functools

import jax
import jax.numpy as jnp
from jax import lax
from jax.experimental import pallas as pl
from jax.experimental.pallas import tpu as pltpu

EPS = 1e-6
LRU_C = 8.0
CONV_W = 4
A_HEADS = 16
B_HD = 128
MLP_CHUNK = 128
MEM_HEADS = 4
N_GROUPS = 4
EXPERTS_PER_GROUP = 8
N_EXPERTS = N_GROUPS * EXPERTS_PER_GROUP
PAST_LEN = 1024

BF16 = jnp.bfloat16
F32 = jnp.float32

V7X_VMEM_BYTES = 64 * 1024 * 1024
V7X_MXU_DIM = 256
SUBLANES = 8

MIX_ROWS = 256
ROUTE_ROWS = 512
EXPERT_ROWS = 512
VMEM_LIMIT = 56 * 1024 * 1024


def _cparams(sem):
    return pltpu.CompilerParams(dimension_semantics=sem, vmem_limit_bytes=VMEM_LIMIT)


def _rms(x, g):
    return x * lax.rsqrt(jnp.mean(x * x, axis=-1, keepdims=True) + EPS) * g


def _dot(a, b):
    return jnp.dot(a, b, preferred_element_type=F32)


def _dot_nt(a, b):
    return lax.dot_general(a, b, (((1,), (1,)), ((), ())), preferred_element_type=F32)


def _full(shape):
    nd = len(shape)
    return pl.BlockSpec(shape, lambda *_: (0,) * nd)


def _gelu_proj(xn, w_ref, lo, width):
    cc = V7X_MXU_DIM
    return jnp.concatenate([jax.nn.gelu(_dot(xn, w_ref[:, lo + c * cc:lo + (c + 1) * cc]))
                            for c in range(width // cc)], axis=-1)


def _mixer_kernel(x_ref, conv0_ref, h0_ref, gin_ref, win_ref, cw_ref, cb_ref, wr_ref, br_ref,
                  wi_ref, bi_ref, lam_ref, vg_ref, ws_ref, bst_ref, goa_ref, gob_ref, wout_ref,
                  *rest, bt, tt, chunk, pos0, emit_vn):
    if emit_vn:
        y_ref, convo_ref, ho_ref, vn_ref = rest[:4]
        scratch = rest[4:]
    else:
        y_ref, convo_ref, ho_ref = rest[:3]
        vn_ref = None
        scratch = rest[3:]
    xa_buf, a_buf, u_buf, h_buf, z_buf, hc_ref, ycat = scratch
    t = pl.program_id(1)
    d = x_ref.shape[-1]
    d_a = a_buf.shape[-1]
    d_b = z_buf.shape[-1]
    m = bt * tt

    @pl.when(t == 0)
    def _():
        xa_buf[:, 5:8, :] = conv0_ref[...]
        hc_ref[...] = h0_ref[...]

    x = x_ref[...].reshape(m, d)
    xn = _rms(x, gin_ref[...]).astype(BF16)
    xa = _dot(xn, win_ref[:, 0:d_a])
    gga = _gelu_proj(xn, win_ref, d_a, d_a)
    gub = _gelu_proj(xn, win_ref, 2 * d_a, d_b)
    gv = _gelu_proj(xn, win_ref, 2 * d_a + d_b, d_b)

    xa_buf[:, 8:8 + tt, :] = xa.reshape(bt, tt, d_a)
    xc = cb_ref[...][None]
    for k in range(CONV_W):
        xc = xc + xa_buf[:, 5 + k:5 + k + tt, :] * cw_ref[k:k + 1, :][None]
    tail = xa_buf[:, tt + 5:tt + 8, :]
    convo_ref[...] = tail
    xa_buf[:, 5:8, :] = tail
    xc = xc.reshape(m, d_a)

    xcb = xc.astype(BF16)
    gw = V7X_MXU_DIM
    r_pre = jnp.concatenate([_dot(xcb[:, g * gw:(g + 1) * gw], wr_ref[g]) for g in range(d_a // gw)], axis=-1)
    i_pre = jnp.concatenate([_dot(xcb[:, g * gw:(g + 1) * gw], wi_ref[g]) for g in range(d_a // gw)], axis=-1)
    r = jax.nn.sigmoid(r_pre + br_ref[...])
    ig = jax.nn.sigmoid(i_pre + bi_ref[...])
    nl = -lam_ref[...]
    softplus = jnp.maximum(nl, 0.0) + jnp.log1p(jnp.exp(-jnp.abs(nl)))
    log_a = -LRU_C * r * softplus
    a = jnp.exp(log_a)
    m2 = -jnp.tanh(log_a) * (1.0 + a * a)
    mult = jnp.where(m2 > 0.0, m2 * lax.rsqrt(m2), 0.0)
    if pos0 == 0:
        row = lax.broadcasted_iota(jnp.int32, (m, d_a), 0)
        is_first = jnp.logical_and(t == 0, (row % tt) == 0)
        mult = jnp.where(is_first, 1.0, mult)
    u = mult * (ig * xc)
    a_buf[...] = a.reshape(bt, tt, d_a)
    u_buf[...] = u.reshape(bt, tt, d_a)

    sub = lax.broadcasted_iota(jnp.int32, (SUBLANES, d_a), 0)
    for b in range(bt):
        h = hc_ref[b]
        for j in range(tt // SUBLANES):
            r0 = j * SUBLANES
            av = a_buf[b, r0:r0 + SUBLANES, :]
            uv = u_buf[b, r0:r0 + SUBLANES, :]
            for s in (1, 2, 4):
                a_sh = jnp.where(sub >= s, pltpu.roll(av, s, 0), 1.0)
                u_sh = jnp.where(sub >= s, pltpu.roll(uv, s, 0), 0.0)
                uv = av * u_sh + uv
                av = av * a_sh
            hv = av * h + uv
            h_buf[b, r0:r0 + SUBLANES, :] = hv
            h = hv[SUBLANES - 1:SUBLANES, :]
        hc_ref[b] = h
    ho_ref[...] = hc_ref[...]

    ya = h_buf[...].reshape(m, d_a) * gga
    ycat[:, 0:d_a] = _rms(ya, goa_ref[...]).astype(BF16)

    heads = d_b // B_HD
    ri = lax.broadcasted_iota(jnp.int32, (chunk, chunk), 0)
    ci = lax.broadcasted_iota(jnp.int32, (chunk, chunk), 1)
    for hh in range(heads):
        lo = hh * B_HD
        gvh = gv[:, lo:lo + B_HD]
        vn_h = gvh * lax.rsqrt(jnp.mean(gvh * gvh, axis=-1, keepdims=True) + EPS) * vg_ref[:, lo:lo + B_HD]
        if emit_vn:
            vn_ref[:, :, lo:lo + B_HD] = vn_h.reshape(bt, tt, B_HD)
        vn_hb = vn_h.astype(BF16)
        ws_h = jnp.where(ci <= ri, ws_ref[hh], 0.0).astype(BF16)
        bias = bst_ref[:, hh:hh + 1]
        for c in range(m // chunk):
            z = _dot(ws_h, vn_hb[c * chunk:(c + 1) * chunk, :]) + bias
            z_buf[c * chunk:(c + 1) * chunk, lo:lo + B_HD] = z
    yb = gub * z_buf[...]
    ycat[:, d_a:] = _rms(yb, gob_ref[...]).astype(BF16)

    y_ref[...] = (x + _dot(ycat[...], wout_ref[...])).reshape(bt, tt, d)


def _mixer(x, conv0, h0, p, *, bt, tt, pos0, emit_vn):
    nb, seq, d = x.shape
    d_a = conv0.shape[-1]
    d_b = p['vg'].shape[-1]
    chunk = min(tt, MLP_CHUNK)
    assert nb % bt == 0 and seq % tt == 0 and tt % chunk == 0 and tt % SUBLANES == 0
    grid = (nb // bt, seq // tt)
    weights = [p['gin'], p['win'], p['cw'], p['cb'], p['wr'], p['br'], p['wi'], p['bi'], p['lam'],
               p['vg'], p['ws'][:, :chunk, :chunk], p['bst'][:chunk], p['goa'], p['gob'], p['wout']]
    in_specs = [pl.BlockSpec((bt, tt, d), lambda b, t: (b, t, 0)),
                pl.BlockSpec((bt, CONV_W - 1, d_a), lambda b, t: (b, 0, 0)),
                pl.BlockSpec((bt, 1, d_a), lambda b, t: (b, 0, 0))]
    in_specs += [_full(w.shape) for w in weights]
    out_shape = [jax.ShapeDtypeStruct((nb, seq, d), F32),
                 jax.ShapeDtypeStruct((nb, CONV_W - 1, d_a), F32),
                 jax.ShapeDtypeStruct((nb, 1, d_a), F32)]
    out_specs = [pl.BlockSpec((bt, tt, d), lambda b, t: (b, t, 0)),
                 pl.BlockSpec((bt, CONV_W - 1, d_a), lambda b, t: (b, 0, 0)),
                 pl.BlockSpec((bt, 1, d_a), lambda b, t: (b, 0, 0))]
    if emit_vn:
        out_shape.append(jax.ShapeDtypeStruct((nb, seq, d_b), F32))
        out_specs.append(pl.BlockSpec((bt, tt, d_b), lambda b, t: (b, t, 0)))
    scratch = [pltpu.VMEM((bt, tt + 8, d_a), F32), pltpu.VMEM((bt, tt, d_a), F32),
               pltpu.VMEM((bt, tt, d_a), F32), pltpu.VMEM((bt, tt, d_a), F32),
               pltpu.VMEM((bt * tt, d_b), F32), pltpu.VMEM((bt, 1, d_a), F32),
               pltpu.VMEM((bt * tt, d), BF16)]
    kern = functools.partial(_mixer_kernel, bt=bt, tt=tt, chunk=chunk, pos0=pos0, emit_vn=emit_vn)
    return pl.pallas_call(
        kern, grid=grid, in_specs=in_specs, out_specs=out_specs, out_shape=out_shape,
        scratch_shapes=scratch, compiler_params=_cparams(("arbitrary", "arbitrary")),
        name="mixer")(x, conv0, h0, *weights)


def _memkv_kernel(m_ref, g_ref, wk_ref, wv_ref, k_ref, v_ref, kb_ref, vb_ref):
    mn = _rms(m_ref[...], g_ref[...]).astype(BF16)
    k = _dot(mn, wk_ref[...])
    v = _dot(mn, wv_ref[...])
    hd = k_ref.shape[-1]
    for hh in range(k_ref.shape[1]):
        k_ref[:, hh, :] = k[:, hh * hd:(hh + 1) * hd]
        v_ref[:, hh, :] = v[:, hh * hd:(hh + 1) * hd]
    kb_ref[...] = k.astype(BF16)
    vb_ref[...] = v.astype(BF16)


def _mem_kv(mem2d, g, wk, wv, *, tm=256):
    n, d = mem2d.shape
    hd = d // MEM_HEADS
    f_spec = pl.BlockSpec((tm, MEM_HEADS, hd), lambda i: (i, 0, 0))
    b_spec = pl.BlockSpec((tm, d), lambda i: (i, 0))
    return pl.pallas_call(
        _memkv_kernel, grid=(n // tm,),
        in_specs=[pl.BlockSpec((tm, d), lambda i: (i, 0)), _full(g.shape), _full(wk.shape), _full(wv.shape)],
        out_specs=[f_spec, f_spec, b_spec, b_spec],
        out_shape=[jax.ShapeDtypeStruct((n, MEM_HEADS, hd), F32), jax.ShapeDtypeStruct((n, MEM_HEADS, hd), F32),
                   jax.ShapeDtypeStruct((n, d), BF16), jax.ShapeDtypeStruct((n, d), BF16)],
        compiler_params=_cparams(("arbitrary",)), name="mem_kv")(mem2d, g, wk, wv)


def _attend_kernel(x_ref, k_ref, v_ref, g_ref, wq_ref, wo_ref, y_ref, o_buf, *, bt, tt):
    d = x_ref.shape[-1]
    hd = d // MEM_HEADS
    x = x_ref[...]
    xn = _rms(x, g_ref[...]).astype(BF16)
    q = _dot(xn, wq_ref[...]).astype(BF16)
    scale = hd ** -0.5
    for b in range(bt):
        for hh in range(MEM_HEADS):
            qh = q[b * tt:(b + 1) * tt, hh * hd:(hh + 1) * hd]
            s = _dot_nt(qh, k_ref[b, :, hh * hd:(hh + 1) * hd]) * scale
            e = jnp.exp(s - jnp.max(s, axis=-1, keepdims=True))
            pr = e / jnp.sum(e, axis=-1, keepdims=True)
            o = _dot(pr.astype(BF16), v_ref[b, :, hh * hd:(hh + 1) * hd])
            o_buf[b * tt:(b + 1) * tt, hh * hd:(hh + 1) * hd] = o.astype(BF16)
    y_ref[...] = x + _dot(o_buf[...], wo_ref[...])


def _attend(x, k, v, g, wq, wo, *, bt, tt):
    nb, seq, d = x.shape
    mt = k.shape[1]
    m = bt * tt
    n = nb * seq
    assert n % m == 0 and (bt == 1 or tt == seq)
    kern = functools.partial(_attend_kernel, bt=bt, tt=tt)
    kv_spec = pl.BlockSpec((bt, mt, d), lambda i: ((i * m) // (bt * seq), 0, 0))
    return pl.pallas_call(
        kern, grid=(n // m,),
        in_specs=[pl.BlockSpec((m, d), lambda i: (i, 0)), kv_spec, kv_spec,
                  _full(g.shape), _full(wq.shape), _full(wo.shape)],
        out_specs=pl.BlockSpec((m, d), lambda i: (i, 0)),
        out_shape=jax.ShapeDtypeStruct((n, d), F32),
        scratch_shapes=[pltpu.VMEM((m, d), BF16)],
        compiler_params=_cparams(("arbitrary",)), name="attend")(x.reshape(n, d), k, v, g, wq, wo)


def _two_source_specs(tt, d, npb):
    return [pl.BlockSpec((tt, d), lambda i, *_: (jnp.minimum(i, npb - 1), 0)),
            pl.BlockSpec((tt, d), lambda i, *_: (jnp.maximum(i - npb, 0), 0))]


def _router_kernel(xp_ref, xs_ref, g_ref, wth_ref, wtl_ref, bias_ref, eid_ref, gate_ref, rank_ref, cnt_ref, *, npb):
    i = pl.program_id(0)
    tt = xp_ref.shape[0]

    @pl.when(i == 0)
    def _():
        cnt_ref[...] = jnp.zeros_like(cnt_ref)

    x = jnp.where(i < npb, xp_ref[...], xs_ref[...])
    xn = _rms(x, g_ref[...])
    xh = xn.astype(BF16)
    xl = (xn - xh.astype(F32)).astype(BF16)
    logits = (_dot_nt(wth_ref[...], xh) + _dot_nt(wth_ref[...], xl) + _dot_nt(wtl_ref[...], xh)) + bias_ref[...]
    lc = logits[0:N_GROUPS]
    ec = jnp.exp(lc - jnp.max(lc, axis=0, keepdims=True))
    pc = ec / jnp.sum(ec, axis=0, keepdims=True)
    p_grp = jnp.max(pc, axis=0, keepdims=True)
    gi = lax.broadcasted_iota(jnp.int32, pc.shape, 0)
    grp = jnp.min(jnp.where(pc == p_grp, gi, N_GROUPS), axis=0, keepdims=True)
    lf = jnp.zeros((EXPERTS_PER_GROUP, tt), F32)
    for gidx in range(N_GROUPS):
        lo = 8 + gidx * EXPERTS_PER_GROUP
        lf = lf + jnp.where(grp == gidx, logits[lo:lo + EXPERTS_PER_GROUP], 0.0)
    fi = lax.broadcasted_iota(jnp.int32, lf.shape, 0)
    t1 = jnp.max(lf, axis=0, keepdims=True)
    i1 = jnp.min(jnp.where(lf == t1, fi, EXPERTS_PER_GROUP), axis=0, keepdims=True)
    lf2 = jnp.where(fi == i1, -jnp.inf, lf)
    t2 = jnp.max(lf2, axis=0, keepdims=True)
    i2 = jnp.min(jnp.where(lf2 == t2, fi, EXPERTS_PER_GROUP), axis=0, keepdims=True)
    e2 = jnp.exp(t2 - t1)
    den = 1.0 + e2
    gate_ref[0:1, :] = p_grp * (1.0 / den)
    gate_ref[1:2, :] = p_grp * (e2 / den)
    eid1 = grp * EXPERTS_PER_GROUP + i1
    eid2 = grp * EXPERTS_PER_GROUP + i2
    eid_ref[0:1, :] = eid1
    eid_ref[1:2, :] = eid2

    ei = lax.broadcasted_iota(jnp.int32, (N_EXPERTS, tt), 0)
    oh1 = ei == eid1
    oh2 = ei == eid2
    oh = jnp.where(jnp.logical_or(oh1, oh2), 1.0, 0.0)
    tr = lax.broadcasted_iota(jnp.int32, (tt, tt), 0)
    tc = lax.broadcasted_iota(jnp.int32, (tt, tt), 1)
    upper = jnp.where(tr < tc, 1.0, 0.0).astype(BF16)
    run = cnt_ref[:, 0:1]
    before = _dot(oh.astype(BF16), upper) + run
    rank_ref[0:1, :] = jnp.sum(jnp.where(oh1, before, 0.0), axis=0, keepdims=True).astype(jnp.int32)
    rank_ref[1:2, :] = jnp.sum(jnp.where(oh2, before, 0.0), axis=0, keepdims=True).astype(jnp.int32)
    cnt_ref[...] = cnt_ref[...] + jnp.sum(oh, axis=1, keepdims=True)


def _router(xp, xs, g, wth, wtl, bias, *, tt):
    (n_p, d), n_s = xp.shape, xs.shape[0]
    npb, n = n_p // tt, n_p + n_s
    tok = pl.BlockSpec((2, tt), lambda i: (0, i))
    return pl.pallas_call(
        functools.partial(_router_kernel, npb=npb), grid=(n // tt,),
        in_specs=_two_source_specs(tt, d, npb) + [_full(g.shape), _full(wth.shape), _full(wtl.shape),
                                                  _full(bias.shape)],
        out_specs=[tok, tok, tok, pl.BlockSpec((N_EXPERTS, 128), lambda i: (0, 0))],
        out_shape=[jax.ShapeDtypeStruct((2, n), jnp.int32),
                   jax.ShapeDtypeStruct((2, n), F32),
                   jax.ShapeDtypeStruct((2, n), jnp.int32),
                   jax.ShapeDtypeStruct((N_EXPERTS, 128), F32)],
        compiler_params=_cparams(("arbitrary",)), name="router")(xp, xs, g, wth, wtl, bias)


def _dispatch_kernel(zs_ref, has_ref, nu_ref, dest_ref, xp_ref, xs_ref, buf_ref, zero_buf, sem, zsem, *, npb, bm):
    i = pl.program_id(0)
    tt = xp_ref.shape[0]
    nblk = buf_ref.shape[0] // bm

    @pl.when(i == 0)
    def _():
        zero_buf[...] = jnp.zeros_like(zero_buf)

        def zcopy(row0):
            return pltpu.make_async_copy(zero_buf, buf_ref.at[pl.ds(pl.multiple_of(row0, bm), bm), :], zsem)

        for e in range(N_EXPERTS):
            @pl.when(has_ref[e] > 0)
            def _():
                zcopy(zs_ref[e]).start()

        def tail_start(blk, c):
            zcopy(blk * bm).start()
            return c

        def tail_wait(blk, c):
            zcopy(blk * bm).wait()
            return c

        lax.fori_loop(nu_ref[0], nblk, tail_start, 0)
        for e in range(N_EXPERTS):
            @pl.when(has_ref[e] > 0)
            def _():
                zcopy(zs_ref[e]).wait()
        lax.fori_loop(nu_ref[0], nblk, tail_wait, 0)

    def scatter(x_ref):
        def group(gidx, c):
            r0 = pl.multiple_of(gidx * SUBLANES, SUBLANES)
            rows = x_ref.at[pl.ds(r0, SUBLANES), :]
            for j in range(SUBLANES):
                for k in range(2):
                    pltpu.make_async_copy(rows.at[pl.ds(j, 1), :],
                                          buf_ref.at[pl.ds(dest_ref[2 * r0 + (2 * j + k)], 1), :], sem).start()
            return c

        lax.fori_loop(0, tt // SUBLANES, group, 0)
        for k in range(2):
            pltpu.make_async_copy(x_ref, buf_ref.at[pl.ds(0, tt), :], sem).wait()

    @pl.when(i < npb)
    def _():
        scatter(xp_ref)

    @pl.when(i >= npb)
    def _():
        scatter(xs_ref)


def _dispatch(zstart, has, n_used, dest, xp, xs, p_rows, *, tt, bm):
    (n_p, d), n_s = xp.shape, xs.shape[0]
    npb, n = n_p // tt, n_p + n_s
    gs = pltpu.PrefetchScalarGridSpec(
        num_scalar_prefetch=3, grid=(n // tt,),
        in_specs=[pl.BlockSpec((2 * tt,), lambda i, *_: (i,), memory_space=pltpu.SMEM)]
        + _two_source_specs(tt, d, npb),
        out_specs=pl.BlockSpec(memory_space=pl.ANY),
        scratch_shapes=[pltpu.VMEM((bm, d), F32), pltpu.SemaphoreType.DMA(()), pltpu.SemaphoreType.DMA(())])
    return pl.pallas_call(
        functools.partial(_dispatch_kernel, npb=npb, bm=bm), grid_spec=gs,
        out_shape=jax.ShapeDtypeStruct((p_rows, d), F32),
        compiler_params=_cparams(("arbitrary",)), name="dispatch")(zstart, has, n_used, dest, xp, xs)


def _experts_kernel(be_ref, nu_ref, x_ref, g_ref, w1_ref, w3_ref, w2_ref, o_ref, w1b, w3b, w2b):
    i = pl.program_id(0)

    @pl.when(i < nu_ref[0])
    def _():
        prev = be_ref[jnp.maximum(i - 1, 0)]

        @pl.when(jnp.logical_or(i == 0, be_ref[i] != prev))
        def _():
            w1b[...] = w1_ref[0].astype(BF16)
            w3b[...] = w3_ref[0].astype(BF16)
            w2b[...] = w2_ref[0].astype(BF16)

        x = x_ref[...]
        xg = (x * g_ref[...]).astype(BF16)
        scale = lax.rsqrt(jnp.mean(x * x, axis=-1, keepdims=True) + EPS)
        h1 = _dot(xg, w1b[...]) * scale
        h3 = _dot(xg, w3b[...]) * scale
        hd = (h1 * jax.nn.sigmoid(h1) * h3).astype(BF16)
        o_ref[...] = _dot(hd, w2b[...])

    @pl.when(i >= nu_ref[0])
    def _():
        o_ref[...] = jnp.zeros_like(o_ref)


def _experts(block_e, n_used, buf, g, w1, w3, w2, *, bm):
    p_rows, d = buf.shape
    ne, _, de = w1.shape
    nblk = p_rows // bm
    gs = pltpu.PrefetchScalarGridSpec(
        num_scalar_prefetch=2, grid=(nblk,),
        in_specs=[pl.BlockSpec((bm, d), lambda i, be, nu: (jnp.minimum(i, nu[0] - 1), 0)),
                  pl.BlockSpec(g.shape, lambda i, be, nu: (0, 0)),
                  pl.BlockSpec((1, d, de), lambda i, be, nu: (be[i], 0, 0)),
                  pl.BlockSpec((1, d, de), lambda i, be, nu: (be[i], 0, 0)),
                  pl.BlockSpec((1, de, d), lambda i, be, nu: (be[i], 0, 0))],
        out_specs=pl.BlockSpec((bm, d), lambda i, be, nu: (i, 0)),
        scratch_shapes=[pltpu.VMEM((d, de), BF16), pltpu.VMEM((d, de), BF16), pltpu.VMEM((de, d), BF16)])
    return pl.pallas_call(
        _experts_kernel, grid_spec=gs,
        out_shape=jax.ShapeDtypeStruct((p_rows, d), F32),
        compiler_params=_cparams(("arbitrary",)), name="experts")(block_e, n_used, buf, g, w1, w3, w2)


def _combine_kernel(dest_ref, x_ref, gate_ref, g_ref, yb_ref, o_ref, a0, b0, a1, b1, sems, *, last):
    i = pl.program_id(0)
    tt = x_ref.shape[0]
    slots = ((a0, b0), (a1, b1))

    def issue(slot):
        for r in range(tt):
            for k in range(2):
                pltpu.make_async_copy(yb_ref.at[pl.ds(dest_ref[2 * r + k], 1), :],
                                      slots[slot][k].at[pl.ds(r, 1), :], sems.at[slot]).start()

    def wait(slot):
        for k in range(2):
            pltpu.make_async_copy(yb_ref.at[pl.ds(0, tt), :], slots[slot][k], sems.at[slot]).wait()

    @pl.when(i == 0)
    def _():
        issue(0)

    for par in range(2):
        @pl.when(jnp.logical_and(i >= 1, i % 2 == par))
        def _(par=par):
            cur = 1 - par
            wait(cur)
            issue(par)
            moe = gate_ref[:, 0:1] * slots[cur][0][...] + gate_ref[:, 1:2] * slots[cur][1][...]
            o_ref[...] = _rms(x_ref[...] + moe, g_ref[...])

            if par == last % 2:
                @pl.when(i == last)
                def _():
                    wait(par)


def _combine(dest_flat, x2d, gate_t, g, yb, *, tt, row_off):
    n, d = x2d.shape
    off = row_off // tt
    nt = n // tt
    prev = lambda i: jnp.maximum(i - 1, 0)
    return pl.pallas_call(
        functools.partial(_combine_kernel, last=nt), grid=(nt + 1,),
        in_specs=[pl.BlockSpec((2 * tt,), lambda i: (jnp.minimum(i, nt - 1) + off,), memory_space=pltpu.SMEM),
                  pl.BlockSpec((tt, d), lambda i: (prev(i), 0)),
                  pl.BlockSpec((tt, 2), lambda i: (prev(i) + off, 0)),
                  _full(g.shape),
                  pl.BlockSpec(memory_space=pl.ANY)],
        out_specs=pl.BlockSpec((tt, d), lambda i: (prev(i), 0)),
        out_shape=jax.ShapeDtypeStruct((n, d), F32),
        scratch_shapes=[pltpu.VMEM((tt, d), F32)] * 4 + [pltpu.SemaphoreType.DMA((2,))],
        compiler_params=_cparams(("arbitrary",)), name="combine")(dest_flat, x2d, gate_t, g, yb)


def _moe_final(xp, xs, mp, g_final):
    (n_p, d), n_s = xp.shape, xs.shape[0]
    n = n_p + n_s
    tt = min(ROUTE_ROWS, n_s)
    assert n_p % tt == 0 and n_s % tt == 0
    eid, gate, rank, cnt = _router(xp, xs, mp['g'], mp['wth'], mp['wtl'], mp['bias'], tt=tt)
    bm = EXPERT_ROWS
    counts = cnt[:, 0].astype(jnp.int32)
    padded = (counts + bm - 1) // bm * bm
    pend = jnp.cumsum(padded)
    pstart = pend - padded
    sel = eid[:, :, None] == jnp.arange(N_EXPERTS, dtype=jnp.int32)
    dest = rank + jnp.sum(jnp.where(sel, pstart, 0), axis=-1)
    dest = dest.T.reshape(-1)
    p_rows = (2 * n + N_EXPERTS * (bm - 1)) // bm * bm
    nblk = p_rows // bm
    block_start = jnp.arange(nblk, dtype=jnp.int32) * bm
    n_used = (pend[-1] // bm).astype(jnp.int32)
    block_e = jnp.sum((pend[None, :] <= block_start[:, None]).astype(jnp.int32), axis=1)
    last_e = jnp.sum((pend <= pend[-1] - 1).astype(jnp.int32))
    block_e = jnp.minimum(block_e, last_e).astype(jnp.int32)
    n_used = n_used.reshape(1)
    buf = _dispatch((pend - bm).astype(jnp.int32), counts, n_used, dest, xp, xs, p_rows, tt=tt, bm=bm)
    yb = _experts(block_e, n_used, buf, mp['g'], mp['w1'], mp['w3'], mp['w2'], bm=bm)
    gate_t = gate.T
    y_p = _combine(dest, xp, gate_t, g_final, yb, tt=tt, row_off=0)
    y_s = _combine(dest, xs, gate_t, g_final, yb, tt=tt, row_off=n_p)
    return y_p, y_s


def _block_diag_pack(w, per):
    h, hd, _ = w.shape
    wg = w.reshape(h // per, per, hd, hd)
    eye = jnp.eye(per, dtype=w.dtype)
    return jnp.einsum('gpij,pq->gpiqj', wg, eye).reshape(h // per, per * hd, per * hd)


def kernel(x_prompt, x_sample, mem_prompt, state_conv_a, state_rglru_h, cache_mem_k, cache_mem_v, norm_mix, w_in, conv_a_w, conv_a_b, lru_w_r, lru_b_r, lru_w_i, lru_b_i, lru_lambda, v_norm_g, mlp_w_s, mlp_b_s, out_norm_a, out_norm_b, w_out, norm_mem_q, norm_mem_kv, w_mem_q, w_mem_k, w_mem_v, w_mem_o, norm_ffn, w_router_c, b_router_c, w_router_f, b_router_f, w_exp_1, w_exp_3, w_exp_2, norm_final):
    depth = w_in.shape[0]
    assert depth == 1
    l = 0
    bp, tp, d = x_prompt.shape
    bs, ts, _ = x_sample.shape
    d_a = conv_a_b.shape[-1]
    a_hd = lru_w_r.shape[-1]
    per = V7X_MXU_DIM // a_hd
    row = lambda v: v.reshape(1, -1)

    mix_p = dict(
        gin=row(norm_mix[l]), win=w_in[l].astype(BF16), cw=conv_a_w[l], cb=row(conv_a_b[l]),
        wr=_block_diag_pack(lru_w_r[l], per).astype(BF16), br=row(lru_b_r[l]),
        wi=_block_diag_pack(lru_w_i[l], per).astype(BF16), bi=row(lru_b_i[l]),
        lam=row(lru_lambda[l]), vg=row(v_norm_g[l]), ws=mlp_w_s[l], bst=mlp_b_s[l].T,
        goa=row(out_norm_a[l]), gob=row(out_norm_b[l]), wout=w_out[l].astype(BF16))

    conv0 = jnp.zeros((bp, CONV_W - 1, d_a), F32)
    h0 = jnp.zeros((bp, 1, d_a), F32)
    tt_p = min(MIX_ROWS, tp)
    x1p, conv_p, h_p = _mixer(x_prompt, conv0, h0, mix_p, bt=1, tt=tt_p, pos0=0, emit_vn=False)
    bt_mix = 8 if bs % 8 == 0 else 1
    x1s, conv_s, h_s, vn_s = _mixer(x_sample, state_conv_a[l], state_rglru_h[l][:, None, :], mix_p,
                                    bt=bt_mix, tt=ts, pos0=PAST_LEN, emit_vn=True)

    mt = mem_prompt.shape[1]
    k_p, v_p, kb_p, vb_p = _mem_kv(mem_prompt.reshape(bp * mt, d), row(norm_mem_kv[l]),
                                   w_mem_k[l].astype(BF16), w_mem_v[l].astype(BF16))
    gq = row(norm_mem_q[l])
    wq = w_mem_q[l].astype(BF16)
    wo = w_mem_o[l].astype(BF16)
    x2p = _attend(x1p, kb_p.reshape(bp, mt, d), vb_p.reshape(bp, mt, d), gq, wq, wo, bt=1, tt=tt_p)
    bt_s = 4 if bs % 4 == 0 else 1
    x2s = _attend(x1s, cache_mem_k[l].reshape(bs, mt, d).astype(BF16),
                  cache_mem_v[l].reshape(bs, mt, d).astype(BF16), gq, wq, wo, bt=bt_s, tt=ts)

    wt = jnp.concatenate([w_router_c[l].T, jnp.zeros((8 - N_GROUPS, d), F32), w_router_f[l].T], axis=0)
    wth = wt.astype(BF16)
    wtl = (wt - wth.astype(F32)).astype(BF16)
    bias = jnp.concatenate([b_router_c[l], jnp.zeros((8 - N_GROUPS,), F32), b_router_f[l]]).reshape(-1, 1)
    moe_p = dict(g=row(norm_ffn[l]), wth=wth, wtl=wtl, bias=bias, w1=w_exp_1[l], w3=w_exp_3[l], w2=w_exp_2[l])
    y_p, y_s = _moe_final(x2p, x2s, moe_p, row(norm_final))

    heads_b = v_norm_g.shape[1]
    hd_m = d // MEM_HEADS
    return (y_p.reshape(bp, tp, d), y_s.reshape(bs, ts, d),
            conv_p[None], h_p.reshape(1, bp, d_a),
            k_p.reshape(1, bp, mt, MEM_HEADS, hd_m), v_p.reshape(1, bp, mt, MEM_HEADS, hd_m),
            conv_s[None], h_s.reshape(1, bs, d_a),
            vn_s.reshape(1, bs, ts, heads_b, B_HD))
```

```python
import functools

import jax
import jax.numpy as jnp
from jax import lax
from jax.experimental import pallas as pl
from jax.experimental.pallas import tpu as pltpu

EPS = 1e-6
LRU_C = 8.0
CONV_W = 4
A_HEADS = 16
B_HD = 128
MLP_CHUNK = 128
MEM_HEADS = 4
N_GROUPS = 4
EXPERTS_PER_GROUP = 8
N_EXPERTS = N_GROUPS * EXPERTS_PER_GROUP
PAST_LEN = 1024

BF16 = jnp.bfloat16
F32 = jnp.float32

V7X_VMEM_BYTES = 64 * 1024 * 1024
V7X_MXU_DIM = 256
SUBLANES = 8

MIX_ROWS = 256
ROUTE_ROWS = 512
EXPERT_ROWS = 512
VMEM_LIMIT = 56 * 1024 * 1024


def _cparams(sem):
    return pltpu.CompilerParams(dimension_semantics=sem, vmem_limit_bytes=VMEM_LIMIT)


def _rms(x, g):
    return x * lax.rsqrt(jnp.mean(x * x, axis=-1, keepdims=True) + EPS) * g


def _dot(a, b):
    return jnp.dot(a, b, preferred_element_type=F32)


def _dot_nt(a, b):
    return lax.dot_general(a, b, (((1,), (1,)), ((), ())), preferred_element_type=F32)


def _full(shape):
    nd = len(shape)
    return pl.BlockSpec(shape, lambda *_: (0,) * nd)


def _gelu_proj(xn, w_ref, lo, width):
    cc = V7X_MXU_DIM
    return jnp.concatenate([jax.nn.gelu(_dot(xn, w_ref[:, lo + c * cc:lo + (c + 1) * cc]))
                            for c in range(width // cc)], axis=-1)


def _mixer_kernel(x_ref, conv0_ref, h0_ref, gin_ref, win_ref, cw_ref, cb_ref, wr_ref, br_ref,
                  wi_ref, bi_ref, lam_ref, vg_ref, ws_ref, bst_ref, goa_ref, gob_ref, wout_ref,
                  *rest, bt, tt, chunk, pos0, emit_vn):
    if emit_vn:
        y_ref, convo_ref, ho_ref, vn_ref = rest[:4]
        scratch = rest[4:]
    else:
        y_ref, convo_ref, ho_ref = rest[:3]
        vn_ref = None
        scratch = rest[3:]
    xa_buf, a_buf, u_buf, h_buf, z_buf, hc_ref, ycat = scratch
    t = pl.program_id(1)
    d = x_ref.shape[-1]
    d_a = a_buf.shape[-1]
    d_b = z_buf.shape[-1]
    m = bt * tt

    @pl.when(t == 0)
    def _():
        xa_buf[:, 5:8, :] = conv0_ref[...]
        hc_ref[...] = h0_ref[...]

    x = x_ref[...].reshape(m, d)
    xn = _rms(x, gin_ref[...]).astype(BF16)
    xa = _dot(xn, win_ref[:, 0:d_a])
    gga = _gelu_proj(xn, win_ref, d_a, d_a)
    gub = _gelu_proj(xn, win_ref, 2 * d_a, d_b)
    gv = _gelu_proj(xn, win_ref, 2 * d_a + d_b, d_b)

    xa_buf[:, 8:8 + tt, :] = xa.reshape(bt, tt, d_a)
    xc = cb_ref[...][None]
    for k in range(CONV_W):
        xc = xc + xa_buf[:, 5 + k:5 + k + tt, :] * cw_ref[k:k + 1, :][None]
    tail = xa_buf[:, tt + 5:tt + 8, :]
    convo_ref[...] = tail
    xa_buf[:, 5:8, :] = tail
    xc = xc.reshape(m, d_a)

    xcb = xc.astype(BF16)
    gw = V7X_MXU_DIM
    r_pre = jnp.concatenate([_dot(xcb[:, g * gw:(g + 1) * gw], wr_ref[g]) for g in range(d_a // gw)], axis=-1)
    i_pre = jnp.concatenate([_dot(xcb[:, g * gw:(g + 1) * gw], wi_ref[g]) for g in range(d_a // gw)], axis=-1)
    r = jax.nn.sigmoid(r_pre + br_ref[...])
    ig = jax.nn.sigmoid(i_pre + bi_ref[...])
    nl = -lam_ref[...]
    softplus = jnp.maximum(nl, 0.0) + jnp.log1p(jnp.exp(-jnp.abs(nl)))
    log_a = -LRU_C * r * softplus
    a = jnp.exp(log_a)
    m2 = -jnp.tanh(log_a) * (1.0 + a * a)
    mult = jnp.where(m2 > 0.0, m2 * lax.rsqrt(m2), 0.0)
    if pos0 == 0:
        row = lax.broadcasted_iota(jnp.int32, (m, d_a), 0)
        is_first = jnp.logical_and(t == 0, (row % tt) == 0)
        mult = jnp.where(is_first, 1.0, mult)
    u = mult * (ig * xc)
    a_buf[...] = a.reshape(bt, tt, d_a)
    u_buf[...] = u.reshape(bt, tt, d_a)

    sub = lax.broadcasted_iota(jnp.int32, (SUBLANES, d_a), 0)
    for b in range(bt):
        h = hc_ref[b]
        for j in range(tt // SUBLANES):
            r0 = j * SUBLANES
            av = a_buf[b, r0:r0 + SUBLANES, :]
            uv = u_buf[b, r0:r0 + SUBLANES, :]
            for s in (1, 2, 4):
                a_sh = jnp.where(sub >= s, pltpu.roll(av, s, 0), 1.0)
                u_sh = jnp.where(sub >= s, pltpu.roll(uv, s, 0), 0.0)
                uv = av * u_sh + uv
                av = av * a_sh
            hv = av * h + uv
            h_buf[b, r0:r0 + SUBLANES, :] = hv
            h = hv[SUBLANES - 1:SUBLANES, :]
        hc_ref[b] = h
    ho_ref[...] = hc_ref[...]

    ya = h_buf[...].reshape(m, d_a) * gga
    ycat[:, 0:d_a] = _rms(ya, goa_ref[...]).astype(BF16)

    heads = d_b // B_HD
    ri = lax.broadcasted_iota(jnp.int32, (chunk, chunk), 0)
    ci = lax.broadcasted_iota(jnp.int32, (chunk, chunk), 1)
    for hh in range(heads):
        lo = hh * B_HD
        gvh = gv[:, lo:lo + B_HD]
        vn_h = gvh * lax.rsqrt(jnp.mean(gvh * gvh, axis=-1, keepdims=True) + EPS) * vg_ref[:, lo:lo + B_HD]
        if emit_vn:
            vn_ref[:, :, lo:lo + B_HD] = vn_h.reshape(bt, tt, B_HD)
        vn_hb = vn_h.astype(BF16)
        ws_h = jnp.where(ci <= ri, ws_ref[hh], 0.0).astype(BF16)
        bias = bst_ref[:, hh:hh + 1]
        for c in range(m // chunk):
            z = _dot(ws_h, vn_hb[c * chunk:(c + 1) * chunk, :]) + bias
            z_buf[c * chunk:(c + 1) * chunk, lo:lo + B_HD] = z
    yb = gub * z_buf[...]
    ycat[:, d_a:] = _rms(yb, gob_ref[...]).astype(BF16)

    y_ref[...] = (x + _dot(ycat[...], wout_ref[...])).reshape(bt, tt, d)


def _mixer(x, conv0, h0, p, *, bt, tt, pos0, emit_vn):
    nb, seq, d = x.shape
    d_a = conv0.shape[-1]
    d_b = p['vg'].shape[-1]
    chunk = min(tt, MLP_CHUNK)
    assert nb % bt == 0 and seq % tt == 0 and tt % chunk == 0 and tt % SUBLANES == 0
    grid = (nb // bt, seq // tt)
    weights = [p['gin'], p['win'], p['cw'], p['cb'], p['wr'], p['br'], p['wi'], p['bi'], p['lam'],
               p['vg'], p['ws'][:, :chunk, :chunk], p['bst'][:chunk], p['goa'], p['gob'], p['wout']]
    in_specs = [pl.BlockSpec((bt, tt, d), lambda b, t: (b, t, 0)),
                pl.BlockSpec((bt, CONV_W - 1, d_a), lambda b, t: (b, 0, 0)),
                pl.BlockSpec((bt, 1, d_a), lambda b, t: (b, 0, 0))]
    in_specs += [_full(w.shape) for w in weights]
    out_shape = [jax.ShapeDtypeStruct((nb, seq, d), F32),
                 jax.ShapeDtypeStruct((nb, CONV_W - 1, d_a), F32),
                 jax.ShapeDtypeStruct((nb, 1, d_a), F32)]
    out_specs = [pl.BlockSpec((bt, tt, d), lambda b, t: (b, t, 0)),
                 pl.BlockSpec((bt, CONV_W - 1, d_a), lambda b, t: (b, 0, 0)),
                 pl.BlockSpec((bt, 1, d_a), lambda b, t: (b, 0, 0))]
    if emit_vn:
        out_shape.append(jax.ShapeDtypeStruct((nb, seq, d_b), F32))
        out_specs.append(pl.BlockSpec((bt, tt, d_b), lambda b, t: (b, t, 0)))
    scratch = [pltpu.VMEM((bt, tt + 8, d_a), F32), pltpu.VMEM((bt, tt, d_a), F32),
               pltpu.VMEM((bt, tt, d_a), F32), pltpu.VMEM((bt, tt, d_a), F32),
               pltpu.VMEM((bt * tt, d_b), F32), pltpu.VMEM((bt, 1, d_a), F32),
               pltpu.VMEM((bt * tt, d), BF16)]
    kern = functools.partial(_mixer_kernel, bt=bt, tt=tt, chunk=chunk, pos0=pos0, emit_vn=emit_vn)
    return pl.pallas_call(
        kern, grid=grid, in_specs=in_specs, out_specs=out_specs, out_shape=out_shape,
        scratch_shapes=scratch, compiler_params=_cparams(("arbitrary", "arbitrary")),
        name="mixer")(x, conv0, h0, *weights)


def _memkv_kernel(m_ref, g_ref, wk_ref, wv_ref, k_ref, v_ref, kb_ref, vb_ref):
    mn = _rms(m_ref[...], g_ref[...]).astype(BF16)
    k = _dot(mn, wk_ref[...])
    v = _dot(mn, wv_ref[...])
    hd = k_ref.shape[-1]
    for hh in range(k_ref.shape[1]):
        k_ref[:, hh, :] = k[:, hh * hd:(hh + 1) * hd]
        v_ref[:, hh, :] = v[:, hh * hd:(hh + 1) * hd]
    kb_ref[...] = k.astype(BF16)
    vb_ref[...] = v.astype(BF16)


def _mem_kv(mem2d, g, wk, wv, *, tm=256):
    n, d = mem2d.shape
    hd = d // MEM_HEADS
    f_spec = pl.BlockSpec((tm, MEM_HEADS, hd), lambda i: (i, 0, 0))
    b_spec = pl.BlockSpec((tm, d), lambda i: (i, 0))
    return pl.pallas_call(
        _memkv_kernel, grid=(n // tm,),
        in_specs=[pl.BlockSpec((tm, d), lambda i: (i, 0)), _full(g.shape), _full(wk.shape), _full(wv.shape)],
        out_specs=[f_spec, f_spec, b_spec, b_spec],
        out_shape=[jax.ShapeDtypeStruct((n, MEM_HEADS, hd), F32), jax.ShapeDtypeStruct((n, MEM_HEADS, hd), F32),
                   jax.ShapeDtypeStruct((n, d), BF16), jax.ShapeDtypeStruct((n, d), BF16)],
        compiler_params=_cparams(("arbitrary",)), name="mem_kv")(mem2d, g, wk, wv)


def _attend_kernel(x_ref, k_ref, v_ref, g_ref, wq_ref, wo_ref, *rest, bt, tt, n_real, n_steps):
    y_ref, o_buf = rest[-2:]
    i = pl.program_id(0)
    d = x_ref.shape[-1]
    hd = d // MEM_HEADS

    @pl.when(i < n_real)
    def _():
        x = x_ref[...]
        xn = _rms(x, g_ref[...]).astype(BF16)
        q = _dot(xn, wq_ref[...]).astype(BF16)
        scale = hd ** -0.5
        for b in range(bt):
            for hh in range(MEM_HEADS):
                qh = q[b * tt:(b + 1) * tt, hh * hd:(hh + 1) * hd]
                s = _dot_nt(qh, k_ref[b, :, hh * hd:(hh + 1) * hd]) * scale
                e = jnp.exp(s - jnp.max(s, axis=-1, keepdims=True))
                pr = e / jnp.sum(e, axis=-1, keepdims=True)
                o = _dot(pr.astype(BF16), v_ref[b, :, hh * hd:(hh + 1) * hd])
                o_buf[b * tt:(b + 1) * tt, hh * hd:(hh + 1) * hd] = o.astype(BF16)
        y_ref[...] = x + _dot(o_buf[...], wo_ref[...])

    if n_steps > n_real:
        @pl.when(i >= n_real)
        def _():
            y_ref[...] = jnp.zeros_like(y_ref)


def _attend(x, k, v, g, wq, wo, *, bt, tt, n_total, row_off, prev=None):
    nb, seq, d = x.shape
    mt = k.shape[1]
    m = bt * tt
    n = nb * seq
    assert n % m == 0 and (bt == 1 or tt == seq) and row_off % m == 0 and n_total % m == 0
    n_real = n // m
    off = row_off // m
    n_steps = n_real if prev is not None else n_total // m - off
    real = lambda i: jnp.minimum(i, n_real - 1)
    kv_spec = pl.BlockSpec((bt, mt, d), lambda i: ((real(i) * m) // (bt * seq), 0, 0))
    in_specs = [pl.BlockSpec((m, d), lambda i: (real(i), 0)), kv_spec, kv_spec,
                _full(g.shape), _full(wq.shape), _full(wo.shape)]
    args = [x.reshape(n, d), k, v, g, wq, wo]
    aliases = {}
    if prev is not None:
        in_specs.append(pl.BlockSpec(memory_space=pl.ANY))
        args.append(prev)
        aliases = {len(args) - 1: 0}
    kern = functools.partial(_attend_kernel, bt=bt, tt=tt, n_real=n_real, n_steps=n_steps)
    return pl.pallas_call(
        kern, grid=(n_steps,), in_specs=in_specs,
        out_specs=pl.BlockSpec((m, d), lambda i: (i + off, 0)),
        out_shape=jax.ShapeDtypeStruct((n_total, d), F32),
        scratch_shapes=[pltpu.VMEM((m, d), BF16)],
        input_output_aliases=aliases,
        compiler_params=_cparams(("arbitrary",)), name="attend")(*args)


def _router_kernel(x_ref, g_ref, wth_ref, wtl_ref, bias_ref, eid_ref, gate_ref, rank_ref, cnt_ref):
    i = pl.program_id(0)
    tt = x_ref.shape[0]

    @pl.when(i == 0)
    def _():
        cnt_ref[...] = jnp.zeros_like(cnt_ref)

    xn = _rms(x_ref[...], g_ref[...])
    xh = xn.astype(BF16)
    xl = (xn - xh.astype(F32)).astype(BF16)
    logits = (_dot_nt(wth_ref[...], xh) + _dot_nt(wth_ref[...], xl) + _dot_nt(wtl_ref[...], xh)) + bias_ref[...]
    lc = logits[0:N_GROUPS]
    ec = jnp.exp(lc - jnp.max(lc, axis=0, keepdims=True))
    pc = ec / jnp.sum(ec, axis=0, keepdims=True)
    p_grp = jnp.max(pc, axis=0, keepdims=True)
    gi = lax.broadcasted_iota(jnp.int32, pc.shape, 0)
    grp = jnp.min(jnp.where(pc == p_grp, gi, N_GROUPS), axis=0, keepdims=True)
    lf = jnp.zeros((EXPERTS_PER_GROUP, tt), F32)
    for gidx in range(N_GROUPS):
        lo = 8 + gidx * EXPERTS_PER_GROUP
        lf = lf + jnp.where(grp == gidx, logits[lo:lo + EXPERTS_PER_GROUP], 0.0)
    fi = lax.broadcasted_iota(jnp.int32, lf.shape, 0)
    t1 = jnp.max(lf, axis=0, keepdims=True)
    i1 = jnp.min(jnp.where(lf == t1, fi, EXPERTS_PER_GROUP), axis=0, keepdims=True)
    lf2 = jnp.where(fi == i1, -jnp.inf, lf)
    t2 = jnp.max(lf2, axis=0, keepdims=True)
    i2 = jnp.min(jnp.where(lf2 == t2, fi, EXPERTS_PER_GROUP), axis=0, keepdims=True)
    e2 = jnp.exp(t2 - t1)
    den = 1.0 + e2
    gate_ref[0:1, :] = p_grp * (1.0 / den)
    gate_ref[1:2, :] = p_grp * (e2 / den)
    eid1 = grp * EXPERTS_PER_GROUP + i1
    eid2 = grp * EXPERTS_PER_GROUP + i2
    eid_ref[0:1, :] = eid1
    eid_ref[1:2, :] = eid2

    ei = lax.broadcasted_iota(jnp.int32, (N_EXPERTS, tt), 0)
    oh1 = ei == eid1
    oh2 = ei == eid2
    oh = jnp.where(jnp.logical_or(oh1, oh2), 1.0, 0.0)
    tr = lax.broadcasted_iota(jnp.int32, (tt, tt), 0)
    tc = lax.broadcasted_iota(jnp.int32, (tt, tt), 1)
    upper = jnp.where(tr < tc, 1.0, 0.0).astype(BF16)
    run = cnt_ref[:, 0:1]
    before = _dot(oh.astype(BF16), upper) + run
    rank_ref[0:1, :] = jnp.sum(jnp.where(oh1, before, 0.0), axis=0, keepdims=True).astype(jnp.int32)
    rank_ref[1:2, :] = jnp.sum(jnp.where(oh2, before, 0.0), axis=0, keepdims=True).astype(jnp.int32)
    cnt_ref[...] = cnt_ref[...] + jnp.sum(oh, axis=1, keepdims=True)


def _router(x, g, wth, wtl, bias, *, tt):
    n, d = x.shape
    tok = pl.BlockSpec((2, tt), lambda i: (0, i))
    return pl.pallas_call(
        _router_kernel, grid=(n // tt,),
        in_specs=[pl.BlockSpec((tt, d), lambda i: (i, 0)), _full(g.shape), _full(wth.shape), _full(wtl.shape),
                  _full(bias.shape)],
        out_specs=[tok, tok, tok, pl.BlockSpec((N_EXPERTS, 128), lambda i: (0, 0))],
        out_shape=[jax.ShapeDtypeStruct((2, n), jnp.int32),
                   jax.ShapeDtypeStruct((2, n), F32),
                   jax.ShapeDtypeStruct((2, n), jnp.int32),
                   jax.ShapeDtypeStruct((N_EXPERTS, 128), F32)],
        compiler_params=_cparams(("arbitrary",)), name="router")(x, g, wth, wtl, bias)


def _invert_kernel(lo_ref, hi_ref, dest_ref, src_ref):
    i = pl.program_id(0)
    n_pairs = dest_ref.shape[0] // 2

    @pl.when(i == 0)
    def _():
        def clear(p, c):
            src_ref[p] = 0
            return c
        for e in range(lo_ref.shape[0]):
            lax.fori_loop(lo_ref[e], hi_ref[e], clear, 0)

    base = i * n_pairs

    def put(r, c):
        src_ref[dest_ref[2 * r]] = base + r
        src_ref[dest_ref[2 * r + 1]] = base + r
        return c

    lax.fori_loop(0, n_pairs, put, 0, unroll=8)


def _invert(pad_lo, pad_hi, dest_flat, p_rows, *, tt):
    n2 = dest_flat.shape[0]
    gs = pltpu.PrefetchScalarGridSpec(
        num_scalar_prefetch=2, grid=(n2 // (2 * tt),),
        in_specs=[pl.BlockSpec((2 * tt,), lambda i, *_: (i,), memory_space=pltpu.SMEM)],
        out_specs=pl.BlockSpec(memory_space=pltpu.SMEM))
    return pl.pallas_call(
        _invert_kernel, grid_spec=gs,
        out_shape=jax.ShapeDtypeStruct((p_rows,), jnp.int32),
        compiler_params=_cparams(("arbitrary",)), name="invert")(pad_lo, pad_hi, dest_flat)


def _experts_kernel(be_ref, nu_ref, src_ref, x_hbm, g_ref, w1_ref, w3_ref, w2_ref, o_ref,
                    xa_buf, xb_buf, sems, w1b, w3b, w2b):
    i = pl.program_id(0)
    nu = nu_ref[0]
    bm = xa_buf.shape[0]
    slots = (xa_buf, xb_buf)

    def issue(blk, slot):
        base = blk * bm
        for r in range(bm):
            pltpu.make_async_copy(x_hbm.at[pl.ds(src_ref[base + r], 1), :],
                                  slots[slot].at[pl.ds(r, 1), :], sems.at[slot]).start()

    def wait(slot):
        pltpu.make_async_copy(x_hbm.at[pl.ds(0, bm), :], slots[slot], sems.at[slot]).wait()

    @pl.when(i == 0)
    def _():
        issue(0, 0)

    for par in range(2):
        @pl.when(jnp.logical_and(i < nu, i % 2 == par))
        def _(par=par):
            prev = be_ref[jnp.maximum(i - 1, 0)]

            @pl.when(jnp.logical_or(i == 0, be_ref[i] != prev))
            def _():
                w1b[...] = w1_ref[0].astype(BF16)
                w3b[...] = w3_ref[0].astype(BF16)
                w2b[...] = w2_ref[0].astype(BF16)

            wait(par)
            issue(jnp.minimum(i + 1, nu - 1), 1 - par)
            x = slots[par][...]
            xg = (x * g_ref[...]).astype(BF16)
            scale = lax.rsqrt(jnp.mean(x * x, axis=-1, keepdims=True) + EPS)
            h1 = _dot(xg, w1b[...]) * scale
            h3 = _dot(xg, w3b[...]) * scale
            hd = (h1 * jax.nn.sigmoid(h1) * h3).astype(BF16)
            o_ref[...] = _dot(hd, w2b[...])

            @pl.when(i == nu - 1)
            def _():
                wait(1 - par)

    @pl.when(i >= nu)
    def _():
        o_ref[...] = jnp.zeros_like(o_ref)


def _experts(block_e, n_used, src, x, g, w1, w3, w2, *, bm):
    p_rows = src.shape[0]
    d = x.shape[-1]
    ne, _, de = w1.shape
    nblk = p_rows // bm
    gs = pltpu.PrefetchScalarGridSpec(
        num_scalar_prefetch=3, grid=(nblk,),
        in_specs=[pl.BlockSpec(memory_space=pl.ANY),
                  pl.BlockSpec(g.shape, lambda i, *_: (0, 0)),
                  pl.BlockSpec((1, d, de), lambda i, be, nu, sr: (be[i], 0, 0)),
                  pl.BlockSpec((1, d, de), lambda i, be, nu, sr: (be[i], 0, 0)),
                  pl.BlockSpec((1, de, d), lambda i, be, nu, sr: (be[i], 0, 0))],
        out_specs=pl.BlockSpec((bm, d), lambda i, *_: (i, 0)),
        scratch_shapes=[pltpu.VMEM((bm, d), F32), pltpu.VMEM((bm, d), F32), pltpu.SemaphoreType.DMA((2,)),
                        pltpu.VMEM((d, de), BF16), pltpu.VMEM((d, de), BF16), pltpu.VMEM((de, d), BF16)])
    return pl.pallas_call(
        _experts_kernel, grid_spec=gs,
        out_shape=jax.ShapeDtypeStruct((p_rows, d), F32),
        compiler_params=_cparams(("arbitrary",)), name="experts")(block_e, n_used, src, x, g, w1, w3, w2)


def _combine_kernel(dest_ref, x_ref, gate_ref, g_ref, yb_ref, o_ref, a0, b0, a1, b1, sems, *, last):
    i = pl.program_id(0)
    tt = x_ref.shape[0]
    slots = ((a0, b0), (a1, b1))

    def issue(slot):
        for r in range(tt):
            for k in range(2):
                pltpu.make_async_copy(yb_ref.at[pl.ds(dest_ref[2 * r + k], 1), :],
                                      slots[slot][k].at[pl.ds(r, 1), :], sems.at[slot]).start()

    def wait(slot):
        for k in range(2):
            pltpu.make_async_copy(yb_ref.at[pl.ds(0, tt), :], slots[slot][k], sems.at[slot]).wait()

    @pl.when(i == 0)
    def _():
        issue(0)

    for par in range(2):
        @pl.when(jnp.logical_and(i >= 1, i % 2 == par))
        def _(par=par):
            cur = 1 - par
            wait(cur)
            issue(par)
            moe = gate_ref[:, 0:1] * slots[cur][0][...] + gate_ref[:, 1:2] * slots[cur][1][...]
            o_ref[...] = _rms(x_ref[...] + moe, g_ref[...])

            if par == last % 2:
                @pl.when(i == last)
                def _():
                    wait(par)


def _combine(dest_flat, x2d, gate_t, g, yb, *, tt, row_off, n):
    d = x2d.shape[-1]
    off = row_off // tt
    nt = n // tt
    prev = lambda i: jnp.maximum(i - 1, 0)
    return pl.pallas_call(
        functools.partial(_combine_kernel, last=nt), grid=(nt + 1,),
        in_specs=[pl.BlockSpec((2 * tt,), lambda i: (jnp.minimum(i, nt - 1) + off,), memory_space=pltpu.SMEM),
                  pl.BlockSpec((tt, d), lambda i: (prev(i) + off, 0)),
                  pl.BlockSpec((tt, 2), lambda i: (prev(i) + off, 0)),
                  _full(g.shape),
                  pl.BlockSpec(memory_space=pl.ANY)],
        out_specs=pl.BlockSpec((tt, d), lambda i: (prev(i), 0)),
        out_shape=jax.ShapeDtypeStruct((n, d), F32),
        scratch_shapes=[pltpu.VMEM((tt, d), F32)] * 4 + [pltpu.SemaphoreType.DMA((2,))],
        compiler_params=_cparams(("arbitrary",)), name="combine")(dest_flat, x2d, gate_t, g, yb)


def _moe_final(x2, n_p, mp, g_final):
    n, d = x2.shape
    n_s = n - n_p
    tt = min(ROUTE_ROWS, n_s)
    assert n_p % tt == 0 and n_s % tt == 0
    eid, gate, rank, cnt = _router(x2, mp['g'], mp['wth'], mp['wtl'], mp['bias'], tt=tt)
    bm = EXPERT_ROWS
    counts = cnt[:, 0].astype(jnp.int32)
    padded = (counts + bm - 1) // bm * bm
    pend = jnp.cumsum(padded)
    pstart = pend - padded
    sel = eid[:, :, None] == jnp.arange(N_EXPERTS, dtype=jnp.int32)
    dest = rank + jnp.sum(jnp.where(sel, pstart, 0), axis=-1)
    dest = dest.T.reshape(-1)
    p_rows = (2 * n + N_EXPERTS * (bm - 1)) // bm * bm
    nblk = p_rows // bm
    block_start = jnp.arange(nblk, dtype=jnp.int32) * bm
    n_used = (pend[-1] // bm).astype(jnp.int32)
    block_e = jnp.sum((pend[None, :] <= block_start[:, None]).astype(jnp.int32), axis=1)
    last_e = jnp.sum((pend <= pend[-1] - 1).astype(jnp.int32))
    block_e = jnp.minimum(block_e, last_e).astype(jnp.int32)
    pad_lo = jnp.concatenate([pstart + counts, pend[-1:]]).astype(jnp.int32)
    pad_hi = jnp.concatenate([pend, jnp.full((1,), p_rows, jnp.int32)]).astype(jnp.int32)
    src = _invert(pad_lo, pad_hi, dest, p_rows, tt=tt)
    yb = _experts(block_e, n_used.reshape(1), src, x2, mp['g'], mp['w1'], mp['w3'], mp['w2'], bm=bm)
    gate_t = gate.T
    y_p = _combine(dest, x2, gate_t, g_final, yb, tt=tt, row_off=0, n=n_p)
    y_s = _combine(dest, x2, gate_t, g_final, yb, tt=tt, row_off=n_p, n=n_s)
    return y_p, y_s


def _block_diag_pack(w, per):
    h, hd, _ = w.shape
    wg = w.reshape(h // per, per, hd, hd)
    eye = jnp.eye(per, dtype=w.dtype)
    return jnp.einsum('gpij,pq->gpiqj', wg, eye).reshape(h // per, per * hd, per * hd)


def kernel(x_prompt, x_sample, mem_prompt, state_conv_a, state_rglru_h, cache_mem_k, cache_mem_v, norm_mix, w_in, conv_a_w, conv_a_b, lru_w_r, lru_b_r, lru_w_i, lru_b_i, lru_lambda, v_norm_g, mlp_w_s, mlp_b_s, out_norm_a, out_norm_b, w_out, norm_mem_q, norm_mem_kv, w_mem_q, w_mem_k, w_mem_v, w_mem_o, norm_ffn, w_router_c, b_router_c, w_router_f, b_router_f, w_exp_1, w_exp_3, w_exp_2, norm_final):
    depth = w_in.shape[0]
    assert depth == 1
    l = 0
    bp, tp, d = x_prompt.shape
    bs, ts, _ = x_sample.shape
    d_a = conv_a_b.shape[-1]
    a_hd = lru_w_r.shape[-1]
    per = V7X_MXU_DIM // a_hd
    row = lambda v: v.reshape(1, -1)

    mix_p = dict(
        gin=row(norm_mix[l]), win=w_in[l].astype(BF16), cw=conv_a_w[l], cb=row(conv_a_b[l]),
        wr=_block_diag_pack(lru_w_r[l], per).astype(BF16), br=row(lru_b_r[l]),
        wi=_block_diag_pack(lru_w_i[l], per).astype(BF16), bi=row(lru_b_i[l]),
        lam=row(lru_lambda[l]), vg=row(v_norm_g[l]), ws=mlp_w_s[l], bst=mlp_b_s[l].T,
        goa=row(out_norm_a[l]), gob=row(out_norm_b[l]), wout=w_out[l].astype(BF16))

    conv0 = jnp.zeros((bp, CONV_W - 1, d_a), F32)
    h0 = jnp.zeros((bp, 1, d_a), F32)
    tt_p = min(MIX_ROWS, tp)
    x1p, conv_p, h_p = _mixer(x_prompt, conv0, h0, mix_p, bt=1, tt=tt_p, pos0=0, emit_vn=False)
    bt_mix = 8 if bs % 8 == 0 else 1
    x1s, conv_s, h_s, vn_s = _mixer(x_sample, state_conv_a[l], state_rglru_h[l][:, None, :], mix_p,
                                    bt=bt_mix, tt=ts, pos0=PAST_LEN, emit_vn=True)

    mt = mem_prompt.shape[1]
    k_p, v_p, kb_p, vb_p = _mem_kv(mem_prompt.reshape(bp * mt, d), row(norm_mem_kv[l]),
                                   w_mem_k[l].astype(BF16), w_mem_v[l].astype(BF16))
    gq = row(norm_mem_q[l])
    wq = w_mem_q[l].astype(BF16)
    wo = w_mem_o[l].astype(BF16)
    n_p, n_tot = bp * tp, bp * tp + bs * ts
    x2 = _attend(x1p, kb_p.reshape(bp, mt, d), vb_p.reshape(bp, mt, d), gq, wq, wo, bt=1, tt=tt_p,
                 n_total=n_tot, row_off=0)
    bt_s = 4 if bs % 4 == 0 else 1
    x2 = _attend(x1s, cache_mem_k[l].reshape(bs, mt, d).astype(BF16),
                 cache_mem_v[l].reshape(bs, mt, d).astype(BF16), gq, wq, wo, bt=bt_s, tt=ts,
                 n_total=n_tot, row_off=n_p, prev=x2)

    wt = jnp.concatenate([w_router_c[l].T, jnp.zeros((8 - N_GROUPS, d), F32), w_router_f[l].T], axis=0)
    wth = wt.astype(BF16)
    wtl = (wt - wth.astype(F32)).astype(BF16)
    bias = jnp.concatenate([b_router_c[l], jnp.zeros((8 - N_GROUPS,), F32), b_router_f[l]]).reshape(-1, 1)
    moe_p = dict(g=row(norm_ffn[l]), wth=wth, wtl=wtl, bias=bias, w1=w_exp_1[l], w3=w_exp_3[l], w2=w_exp_2[l])
    y_p, y_s = _moe_final(x2, n_p, moe_p, row(norm_final))

    heads_b = v_norm_g.shape[1]
    hd_m = d // MEM_HEADS
    return (y_p.reshape(bp, tp, d), y_s.reshape(bs, ts, d),
            conv_p[None], h_p.reshape(1, bp, d_a),
            k_p.reshape(1, bp, mt, MEM_HEADS, hd_m), v_p.reshape(1, bp, mt, MEM_HEADS, hd_m),
            conv_s[None], h_s.reshape(1, bs, d_a),
            vn_s.reshape(1, bs, ts, heads_b, B_HD))
```

```python
import functools

import jax
import jax.numpy as jnp
from jax import lax
from jax.experimental import pallas as pl
from jax.experimental.pallas import tpu as pltpu

EPS = 1e-6
LRU_C = 8.0
CONV_W = 4
A_HEADS = 16
B_HD = 128
MLP_CHUNK = 128
MEM_HEADS = 4
N_GROUPS = 4
EXPERTS_PER_GROUP = 8
N_EXPERTS = N_GROUPS * EXPERTS_PER_GROUP
PAST_LEN = 1024

BF16 = jnp.bfloat16
F32 = jnp.float32

V7X_VMEM_BYTES = 64 * 1024 * 1024
V7X_MXU_DIM = 256
SUBLANES = 8

MIX_ROWS = 256
ROUTE_ROWS = 512
EXPERT_ROWS = 512
VMEM_LIMIT = 56 * 1024 * 1024


def _cparams(sem):
    return pltpu.CompilerParams(dimension_semantics=sem, vmem_limit_bytes=VMEM_LIMIT)


def _rms(x, g):
    return x * lax.rsqrt(jnp.mean(x * x, axis=-1, keepdims=True) + EPS) * g


def _dot(a, b):
    return jnp.dot(a, b, preferred_element_type=F32)


def _dot_nt(a, b):
    return lax.dot_general(a, b, (((1,), (1,)), ((), ())), preferred_element_type=F32)


def _full(shape):
    nd = len(shape)
    return pl.BlockSpec(shape, lambda *_: (0,) * nd)


def _gelu_proj(xn, w_ref, lo, width):
    cc = V7X_MXU_DIM
    return jnp.concatenate([jax.nn.gelu(_dot(xn, w_ref[:, lo + c * cc:lo + (c + 1) * cc]))
                            for c in range(width // cc)], axis=-1)


def _mixer_kernel(x_ref, conv0_ref, h0_ref, gin_ref, win_ref, cw_ref, cb_ref, wr_ref, br_ref,
                  wi_ref, bi_ref, lam_ref, vg_ref, ws_ref, bst_ref, goa_ref, gob_ref, wout_ref,
                  *rest, bt, tt, chunk, pos0, emit_vn):
    if emit_vn:
        y_ref, convo_ref, ho_ref, vn_ref = rest[:4]
        scratch = rest[4:]
    else:
        y_ref, convo_ref, ho_ref = rest[:3]
        vn_ref = None
        scratch = rest[3:]
    xa_buf, a_buf, u_buf, h_buf, z_buf, hc_ref, ycat = scratch
    t = pl.program_id(1)
    d = x_ref.shape[-1]
    d_a = a_buf.shape[-1]
    d_b = z_buf.shape[-1]
    m = bt * tt

    @pl.when(t == 0)
    def _():
        xa_buf[:, 5:8, :] = conv0_ref[...]
        hc_ref[...] = h0_ref[...]

    x = x_ref[...].reshape(m, d)
    xn = _rms(x, gin_ref[...]).astype(BF16)
    xa = _dot(xn, win_ref[:, 0:d_a])
    gga = _gelu_proj(xn, win_ref, d_a, d_a)
    gub = _gelu_proj(xn, win_ref, 2 * d_a, d_b)
    gv = _gelu_proj(xn, win_ref, 2 * d_a + d_b, d_b)

    xa_buf[:, 8:8 + tt, :] = xa.reshape(bt, tt, d_a)
    xc = cb_ref[...][None]
    for k in range(CONV_W):
        xc = xc + xa_buf[:, 5 + k:5 + k + tt, :] * cw_ref[k:k + 1, :][None]
    tail = xa_buf[:, tt + 5:tt + 8, :]
    convo_ref[...] = tail
    xa_buf[:, 5:8, :] = tail
    xc = xc.reshape(m, d_a)

    xcb = xc.astype(BF16)
    gw = V7X_MXU_DIM
    r_pre = jnp.concatenate([_dot(xcb[:, g * gw:(g + 1) * gw], wr_ref[g]) for g in range(d_a // gw)], axis=-1)
    i_pre = jnp.concatenate([_dot(xcb[:, g * gw:(g + 1) * gw], wi_ref[g]) for g in range(d_a // gw)], axis=-1)
    r = jax.nn.sigmoid(r_pre + br_ref[...])
    ig = jax.nn.sigmoid(i_pre + bi_ref[...])
    nl = -lam_ref[...]
    softplus = jnp.maximum(nl, 0.0) + jnp.log1p(jnp.exp(-jnp.abs(nl)))
    log_a = -LRU_C * r * softplus
    a = jnp.exp(log_a)
    m2 = -jnp.tanh(log_a) * (1.0 + a * a)
    mult = jnp.where(m2 > 0.0, m2 * lax.rsqrt(m2), 0.0)
    if pos0 == 0:
        row = lax.broadcasted_iota(jnp.int32, (m, d_a), 0)
        is_first = jnp.logical_and(t == 0, (row % tt) == 0)
        mult = jnp.where(is_first, 1.0, mult)
    u = mult * (ig * xc)
    a_buf[...] = a.reshape(bt, tt, d_a)
    u_buf[...] = u.reshape(bt, tt, d_a)

    sub = lax.broadcasted_iota(jnp.int32, (SUBLANES, d_a), 0)
    for b in range(bt):
        h = hc_ref[b]
        for j in range(tt // SUBLANES):
            r0 = j * SUBLANES
            av = a_buf[b, r0:r0 + SUBLANES, :]
            uv = u_buf[b, r0:r0 + SUBLANES, :]
            for s in (1, 2, 4):
                a_sh = jnp.where(sub >= s, pltpu.roll(av, s, 0), 1.0)
                u_sh = jnp.where(sub >= s, pltpu.roll(uv, s, 0), 0.0)
                uv = av * u_sh + uv
                av = av * a_sh
            hv = av * h + uv
            h_buf[b, r0:r0 + SUBLANES, :] = hv
            h = hv[SUBLANES - 1:SUBLANES, :]
        hc_ref[b] = h
    ho_ref[...] = hc_ref[...]

    ya = h_buf[...].reshape(m, d_a) * gga
    ycat[:, 0:d_a] = _rms(ya, goa_ref[...]).astype(BF16)

    heads = d_b // B_HD
    ri = lax.broadcasted_iota(jnp.int32, (chunk, chunk), 0)
    ci = lax.broadcasted_iota(jnp.int32, (chunk, chunk), 1)
    for hh in range(heads):
        lo = hh * B_HD
        gvh = gv[:, lo:lo + B_HD]
        vn_h = gvh * lax.rsqrt(jnp.mean(gvh * gvh, axis=-1, keepdims=True) + EPS) * vg_ref[:, lo:lo + B_HD]
        if emit_vn:
            vn_ref[:, :, lo:lo + B_HD] = vn_h.reshape(bt, tt, B_HD)
        vn_hb = vn_h.astype(BF16)
        ws_h = jnp.where(ci <= ri, ws_ref[hh], 0.0).astype(BF16)
        bias = bst_ref[:, hh:hh + 1]
        for c in range(m // chunk):
            z = _dot(ws_h, vn_hb[c * chunk:(c + 1) * chunk, :]) + bias
            z_buf[c * chunk:(c + 1) * chunk, lo:lo + B_HD] = z
    yb = gub * z_buf[...]
    ycat[:, d_a:] = _rms(yb, gob_ref[...]).astype(BF16)

    y_ref[...] = (x + _dot(ycat[...], wout_ref[...])).reshape(bt, tt, d)


def _mixer(x, conv0, h0, p, *, bt, tt, pos0, emit_vn):
    nb, seq, d = x.shape
    d_a = conv0.shape[-1]
    d_b = p['vg'].shape[-1]
    chunk = min(tt, MLP_CHUNK)
    assert nb % bt == 0 and seq % tt == 0 and tt % chunk == 0 and tt % SUBLANES == 0
    grid = (nb // bt, seq // tt)
    weights = [p['gin'], p['win'], p['cw'], p['cb'], p['wr'], p['br'], p['wi'], p['bi'], p['lam'],
               p['vg'], p['ws'][:, :chunk, :chunk], p['bst'][:chunk], p['goa'], p['gob'], p['wout']]
    in_specs = [pl.BlockSpec((bt, tt, d), lambda b, t: (b, t, 0)),
                pl.BlockSpec((bt, CONV_W - 1, d_a), lambda b, t: (b, 0, 0)),
                pl.BlockSpec((bt, 1, d_a), lambda b, t: (b, 0, 0))]
    in_specs += [_full(w.shape) for w in weights]
    out_shape = [jax.ShapeDtypeStruct((nb, seq, d), F32),
                 jax.ShapeDtypeStruct((nb, CONV_W - 1, d_a), F32),
                 jax.ShapeDtypeStruct((nb, 1, d_a), F32)]
    out_specs = [pl.BlockSpec((bt, tt, d), lambda b, t: (b, t, 0)),
                 pl.BlockSpec((bt, CONV_W - 1, d_a), lambda b, t: (b, 0, 0)),
                 pl.BlockSpec((bt, 1, d_a), lambda b, t: (b, 0, 0))]
    if emit_vn:
        out_shape.append(jax.ShapeDtypeStruct((nb, seq, d_b), F32))
        out_specs.append(pl.BlockSpec((bt, tt, d_b), lambda b, t: (b, t, 0)))
    scratch = [pltpu.VMEM((bt, tt + 8, d_a), F32), pltpu.VMEM((bt, tt, d_a), F32),
               pltpu.VMEM((bt, tt, d_a), F32), pltpu.VMEM((bt, tt, d_a), F32),
               pltpu.VMEM((bt * tt, d_b), F32), pltpu.VMEM((bt, 1, d_a), F32),
               pltpu.VMEM((bt * tt, d), BF16)]
    kern = functools.partial(_mixer_kernel, bt=bt, tt=tt, chunk=chunk, pos0=pos0, emit_vn=emit_vn)
    return pl.pallas_call(
        kern, grid=grid, in_specs=in_specs, out_specs=out_specs, out_shape=out_shape,
        scratch_shapes=scratch, compiler_params=_cparams(("arbitrary", "arbitrary")),
        name="mixer")(x, conv0, h0, *weights)


def _memkv_kernel(m_ref, g_ref, wk_ref, wv_ref, k_ref, v_ref, kb_ref, vb_ref):
    mn = _rms(m_ref[...], g_ref[...]).astype(BF16)
    k = _dot(mn, wk_ref[...])
    v = _dot(mn, wv_ref[...])
    hd = k_ref.shape[-1]
    for hh in range(k_ref.shape[1]):
        k_ref[:, hh, :] = k[:, hh * hd:(hh + 1) * hd]
        v_ref[:, hh, :] = v[:, hh * hd:(hh + 1) * hd]
    kb_ref[...] = k.astype(BF16)
    vb_ref[...] = v.astype(BF16)


def _mem_kv(mem2d, g, wk, wv, *, tm=256):
    n, d = mem2d.shape
    hd = d // MEM_HEADS
    f_spec = pl.BlockSpec((tm, MEM_HEADS, hd), lambda i: (i, 0, 0))
    b_spec = pl.BlockSpec((tm, d), lambda i: (i, 0))
    return pl.pallas_call(
        _memkv_kernel, grid=(n // tm,),
        in_specs=[pl.BlockSpec((tm, d), lambda i: (i, 0)), _full(g.shape), _full(wk.shape), _full(wv.shape)],
        out_specs=[f_spec, f_spec, b_spec, b_spec],
        out_shape=[jax.ShapeDtypeStruct((n, MEM_HEADS, hd), F32), jax.ShapeDtypeStruct((n, MEM_HEADS, hd), F32),
                   jax.ShapeDtypeStruct((n, d), BF16), jax.ShapeDtypeStruct((n, d), BF16)],
        compiler_params=_cparams(("arbitrary",)), name="mem_kv")(mem2d, g, wk, wv)


def _attend_kernel(x_ref, k_ref, v_ref, g_ref, wq_ref, wo_ref, y_ref, o_buf, *, bt, tt):
    d = x_ref.shape[-1]
    hd = d // MEM_HEADS
    x = x_ref[...]
    xn = _rms(x, g_ref[...]).astype(BF16)
    q = _dot(xn, wq_ref[...]).astype(BF16)
    scale = hd ** -0.5
    for b in range(bt):
        for hh in range(MEM_HEADS):
            qh = q[b * tt:(b + 1) * tt, hh * hd:(hh + 1) * hd]
            s = _dot_nt(qh, k_ref[b, :, hh * hd:(hh + 1) * hd]) * scale
            e = jnp.exp(s - jnp.max(s, axis=-1, keepdims=True))
            pr = e / jnp.sum(e, axis=-1, keepdims=True)
            o = _dot(pr.astype(BF16), v_ref[b, :, hh * hd:(hh + 1) * hd])
            o_buf[b * tt:(b + 1) * tt, hh * hd:(hh + 1) * hd] = o.astype(BF16)
    y_ref[...] = x + _dot(o_buf[...], wo_ref[...])


def _attend(x, k, v, g, wq, wo, *, bt, tt):
    nb, seq, d = x.shape
    mt = k.shape[1]
    m = bt * tt
    n = nb * seq
    assert n % m == 0 and (bt == 1 or tt == seq)
    kern = functools.partial(_attend_kernel, bt=bt, tt=tt)
    kv_spec = pl.BlockSpec((bt, mt, d), lambda i: ((i * m) // (bt * seq), 0, 0))
    return pl.pallas_call(
        kern, grid=(n // m,),
        in_specs=[pl.BlockSpec((m, d), lambda i: (i, 0)), kv_spec, kv_spec,
                  _full(g.shape), _full(wq.shape), _full(wo.shape)],
        out_specs=pl.BlockSpec((m, d), lambda i: (i, 0)),
        out_shape=jax.ShapeDtypeStruct((n, d), F32),
        scratch_shapes=[pltpu.VMEM((m, d), BF16)],
        compiler_params=_cparams(("arbitrary",)), name="attend")(x.reshape(n, d), k, v, g, wq, wo)


def _two_source_specs(tt, d, npb):
    return [pl.BlockSpec((tt, d), lambda i, *_: (jnp.minimum(i, npb - 1), 0)),
            pl.BlockSpec((tt, d), lambda i, *_: (jnp.maximum(i - npb, 0), 0))]


def _router_kernel(xp_ref, xs_ref, g_ref, wth_ref, wtl_ref, bias_ref, eid_ref, gate_ref, rank_ref, cnt_ref, *, npb):
    i = pl.program_id(0)
    tt = xp_ref.shape[0]

    @pl.when(i == 0)
    def _():
        cnt_ref[...] = jnp.zeros_like(cnt_ref)

    x = jnp.where(i < npb, xp_ref[...], xs_ref[...])
    xn = _rms(x, g_ref[...])
    xh = xn.astype(BF16)
    xl = (xn - xh.astype(F32)).astype(BF16)
    logits = (_dot_nt(wth_ref[...], xh) + _dot_nt(wth_ref[...], xl) + _dot_nt(wtl_ref[...], xh)) + bias_ref[...]
    lc = logits[0:N_GROUPS]
    ec = jnp.exp(lc - jnp.max(lc, axis=0, keepdims=True))
    pc = ec / jnp.sum(ec, axis=0, keepdims=True)
    p_grp = jnp.max(pc, axis=0, keepdims=True)
    gi = lax.broadcasted_iota(jnp.int32, pc.shape, 0)
    grp = jnp.min(jnp.where(pc == p_grp, gi, N_GROUPS), axis=0, keepdims=True)
    lf = jnp.zeros((EXPERTS_PER_GROUP, tt), F32)
    for gidx in range(N_GROUPS):
        lo = 8 + gidx * EXPERTS_PER_GROUP
        lf = lf + jnp.where(grp == gidx, logits[lo:lo + EXPERTS_PER_GROUP], 0.0)
    fi = lax.broadcasted_iota(jnp.int32, lf.shape, 0)
    t1 = jnp.max(lf, axis=0, keepdims=True)
    i1 = jnp.min(jnp.where(lf == t1, fi, EXPERTS_PER_GROUP), axis=0, keepdims=True)
    lf2 = jnp.where(fi == i1, -jnp.inf, lf)
    t2 = jnp.max(lf2, axis=0, keepdims=True)
    i2 = jnp.min(jnp.where(lf2 == t2, fi, EXPERTS_PER_GROUP), axis=0, keepdims=True)
    e2 = jnp.exp(t2 - t1)
    den = 1.0 + e2
    gate_ref[0:1, :] = p_grp * (1.0 / den)
    gate_ref[1:2, :] = p_grp * (e2 / den)
    eid1 = grp * EXPERTS_PER_GROUP + i1
    eid2 = grp * EXPERTS_PER_GROUP + i2
    eid_ref[0:1, :] = eid1
    eid_ref[1:2, :] = eid2

    ei = lax.broadcasted_iota(jnp.int32, (N_EXPERTS, tt), 0)
    oh1 = ei == eid1
    oh2 = ei == eid2
    oh = jnp.where(jnp.logical_or(oh1, oh2), 1.0, 0.0)
    tr = lax.broadcasted_iota(jnp.int32, (tt, tt), 0)
    tc = lax.broadcasted_iota(jnp.int32, (tt, tt), 1)
    upper = jnp.where(tr < tc, 1.0, 0.0).astype(BF16)
    run = cnt_ref[:, 0:1]
    before = _dot(oh.astype(BF16), upper) + run
    rank_ref[0:1, :] = jnp.sum(jnp.where(oh1, before, 0.0), axis=0, keepdims=True).astype(jnp.int32)
    rank_ref[1:2, :] = jnp.sum(jnp.where(oh2, before, 0.0), axis=0, keepdims=True).astype(jnp.int32)
    cnt_ref[...] = cnt_ref[...] + jnp.sum(oh, axis=1, keepdims=True)


def _router(xp, xs, g, wth, wtl, bias, *, tt):
    (n_p, d), n_s = xp.shape, xs.shape[0]
    npb, n = n_p // tt, n_p + n_s
    tok = pl.BlockSpec((2, tt), lambda i: (0, i))
    return pl.pallas_call(
        functools.partial(_router_kernel, npb=npb), grid=(n // tt,),
        in_specs=_two_source_specs(tt, d, npb) + [_full(g.shape), _full(wth.shape), _full(wtl.shape),
                                                  _full(bias.shape)],
        out_specs=[tok, tok, tok, pl.BlockSpec((N_EXPERTS, 128), lambda i: (0, 0))],
        out_shape=[jax.ShapeDtypeStruct((2, n), jnp.int32),
                   jax.ShapeDtypeStruct((2, n), F32),
                   jax.ShapeDtypeStruct((2, n), jnp.int32),
                   jax.ShapeDtypeStruct((N_EXPERTS, 128), F32)],
        compiler_params=_cparams(("arbitrary",)), name="router")(xp, xs, g, wth, wtl, bias)


def _dispatch_kernel(zs_ref, has_ref, nu_ref, dest_ref, xp_hbm, xs_hbm, buf_ref, zero_buf, xbuf, fsem, ssem, zsem,
                     *, npb, nt, bm):
    i = pl.program_id(0)
    tt = xbuf.shape[1]
    nblk = buf_ref.shape[0] // bm

    def fetch(t, slot):
        @pl.when(t < npb)
        def _():
            pltpu.make_async_copy(xp_hbm.at[pl.ds(pl.multiple_of(t * tt, tt), tt), :],
                                  xbuf.at[slot], fsem.at[slot]).start()

        @pl.when(t >= npb)
        def _():
            pltpu.make_async_copy(xs_hbm.at[pl.ds(pl.multiple_of((t - npb) * tt, tt), tt), :],
                                  xbuf.at[slot], fsem.at[slot]).start()

    def fetch_wait(slot):
        pltpu.make_async_copy(xp_hbm.at[pl.ds(0, tt), :], xbuf.at[slot], fsem.at[slot]).wait()

    def scatter_wait(slot):
        for k in range(2):
            pltpu.make_async_copy(xbuf.at[slot], buf_ref.at[pl.ds(0, tt), :], ssem.at[slot]).wait()

    @pl.when(i == 0)
    def _():
        zero_buf[...] = jnp.zeros_like(zero_buf)

        def zcopy(row0):
            return pltpu.make_async_copy(zero_buf, buf_ref.at[pl.ds(pl.multiple_of(row0, bm), bm), :], zsem)

        for e in range(N_EXPERTS):
            @pl.when(has_ref[e] > 0)
            def _():
                zcopy(zs_ref[e]).start()

        def tail_start(blk, c):
            zcopy(blk * bm).start()
            return c

        def tail_wait(blk, c):
            zcopy(blk * bm).wait()
            return c

        lax.fori_loop(nu_ref[0], nblk, tail_start, 0)
        for e in range(N_EXPERTS):
            @pl.when(has_ref[e] > 0)
            def _():
                zcopy(zs_ref[e]).wait()
        lax.fori_loop(nu_ref[0], nblk, tail_wait, 0)
        fetch(0, 0)

    slot = i % 3
    nxt = (i + 1) % 3

    @pl.when(i >= 2)
    def _():
        scatter_wait(nxt)

    @pl.when(i + 1 < nt)
    def _():
        fetch(i + 1, nxt)

    fetch_wait(slot)
    x_ref = xbuf.at[slot]
    s_sem = ssem.at[slot]

    def group(gidx, c):
        r0 = pl.multiple_of(gidx * SUBLANES, SUBLANES)
        rows = x_ref.at[pl.ds(r0, SUBLANES), :]
        for j in range(SUBLANES):
            for k in range(2):
                pltpu.make_async_copy(rows.at[pl.ds(j, 1), :],
                                      buf_ref.at[pl.ds(dest_ref[2 * r0 + (2 * j + k)], 1), :], s_sem).start()
        return c

    lax.fori_loop(0, tt // SUBLANES, group, 0)

    @pl.when(i == nt - 1)
    def _():
        if nt > 1:
            scatter_wait((i + 2) % 3)
        scatter_wait(slot)


def _dispatch(zstart, has, n_used, dest, xp, xs, p_rows, *, tt, bm):
    (n_p, d), n_s = xp.shape, xs.shape[0]
    npb, n = n_p // tt, n_p + n_s
    nt = n // tt
    gs = pltpu.PrefetchScalarGridSpec(
        num_scalar_prefetch=3, grid=(nt,),
        in_specs=[pl.BlockSpec((2 * tt,), lambda i, *_: (i,), memory_space=pltpu.SMEM),
                  pl.BlockSpec(memory_space=pl.ANY), pl.BlockSpec(memory_space=pl.ANY)],
        out_specs=pl.BlockSpec(memory_space=pl.ANY),
        scratch_shapes=[pltpu.VMEM((bm, d), F32), pltpu.VMEM((3, tt, d), F32),
                        pltpu.SemaphoreType.DMA((3,)), pltpu.SemaphoreType.DMA((3,)),
                        pltpu.SemaphoreType.DMA(())])
    return pl.pallas_call(
        functools.partial(_dispatch_kernel, npb=npb, nt=nt, bm=bm), grid_spec=gs,
        out_shape=jax.ShapeDtypeStruct((p_rows, d), F32),
        compiler_params=_cparams(("arbitrary",)), name="dispatch")(zstart, has, n_used, dest, xp, xs)


def _experts_kernel(be_ref, nu_ref, x_ref, g_ref, w1_ref, w3_ref, w2_ref, o_ref, w1b, w3b, w2b):
    i = pl.program_id(0)

    @pl.when(i < nu_ref[0])
    def _():
        prev = be_ref[jnp.maximum(i - 1, 0)]

        @pl.when(jnp.logical_or(i == 0, be_ref[i] != prev))
        def _():
            w1b[...] = w1_ref[0].astype(BF16)
            w3b[...] = w3_ref[0].astype(BF16)
            w2b[...] = w2_ref[0].astype(BF16)

        x = x_ref[...]
        xg = (x * g_ref[...]).astype(BF16)
        scale = lax.rsqrt(jnp.mean(x * x, axis=-1, keepdims=True) + EPS)
        h1 = _dot(xg, w1b[...]) * scale
        h3 = _dot(xg, w3b[...]) * scale
        hd = (h1 * jax.nn.sigmoid(h1) * h3).astype(BF16)
        o_ref[...] = _dot(hd, w2b[...])

    @pl.when(i >= nu_ref[0])
    def _():
        o_ref[...] = jnp.zeros_like(o_ref)


def _experts(block_e, n_used, buf, g, w1, w3, w2, *, bm):
    p_rows, d = buf.shape
    ne, _, de = w1.shape
    nblk = p_rows // bm
    gs = pltpu.PrefetchScalarGridSpec(
        num_scalar_prefetch=2, grid=(nblk,),
        in_specs=[pl.BlockSpec((bm, d), lambda i, be, nu: (jnp.minimum(i, nu[0] - 1), 0)),
                  pl.BlockSpec(g.shape, lambda i, be, nu: (0, 0)),
                  pl.BlockSpec((1, d, de), lambda i, be, nu: (be[i], 0, 0)),
                  pl.BlockSpec((1, d, de), lambda i, be, nu: (be[i], 0, 0)),
                  pl.BlockSpec((1, de, d), lambda i, be, nu: (be[i], 0, 0))],
        out_specs=pl.BlockSpec((bm, d), lambda i, be, nu: (i, 0)),
        scratch_shapes=[pltpu.VMEM((d, de), BF16), pltpu.VMEM((d, de), BF16), pltpu.VMEM((de, d), BF16)])
    return pl.pallas_call(
        _experts_kernel, grid_spec=gs,
        out_shape=jax.ShapeDtypeStruct((p_rows, d), F32),
        compiler_params=_cparams(("arbitrary",)), name="experts")(block_e, n_used, buf, g, w1, w3, w2)


def _combine_kernel(dest_ref, x_ref, gate_ref, g_ref, yb_ref, o_ref, a0, b0, a1, b1, sems, *, last):
    i = pl.program_id(0)
    tt = x_ref.shape[0]
    slots = ((a0, b0), (a1, b1))

    def issue(slot):
        for r in range(tt):
            for k in range(2):
                pltpu.make_async_copy(yb_ref.at[pl.ds(dest_ref[2 * r + k], 1), :],
                                      slots[slot][k].at[pl.ds(r, 1), :], sems.at[slot]).start()

    def wait(slot):
        for k in range(2):
            pltpu.make_async_copy(yb_ref.at[pl.ds(0, tt), :], slots[slot][k], sems.at[slot]).wait()

    @pl.when(i == 0)
    def _():
        issue(0)

    for par in range(2):
        @pl.when(jnp.logical_and(i >= 1, i % 2 == par))
        def _(par=par):
            cur = 1 - par
            wait(cur)
            issue(par)
            moe = gate_ref[:, 0:1] * slots[cur][0][...] + gate_ref[:, 1:2] * slots[cur][1][...]
            o_ref[...] = _rms(x_ref[...] + moe, g_ref[...])

            if par == last % 2:
                @pl.when(i == last)
                def _():
                    wait(par)


def _combine(dest_flat, x2d, gate_t, g, yb, *, tt, row_off):
    n, d = x2d.shape
    off = row_off // tt
    nt = n // tt
    prev = lambda i: jnp.maximum(i - 1, 0)
    return pl.pallas_call(
        functools.partial(_combine_kernel, last=nt), grid=(nt + 1,),
        in_specs=[pl.BlockSpec((2 * tt,), lambda i: (jnp.minimum(i, nt - 1) + off,), memory_space=pltpu.SMEM),
                  pl.BlockSpec((tt, d), lambda i: (prev(i), 0)),
                  pl.BlockSpec((tt, 2), lambda i: (prev(i) + off, 0)),
                  _full(g.shape),
                  pl.BlockSpec(memory_space=pl.ANY)],
        out_specs=pl.BlockSpec((tt, d), lambda i: (prev(i), 0)),
        out_shape=jax.ShapeDtypeStruct((n, d), F32),
        scratch_shapes=[pltpu.VMEM((tt, d), F32)] * 4 + [pltpu.SemaphoreType.DMA((2,))],
        compiler_params=_cparams(("arbitrary",)), name="combine")(dest_flat, x2d, gate_t, g, yb)


def _moe_final(xp, xs, mp, g_final):
    (n_p, d), n_s = xp.shape, xs.shape[0]
    n = n_p + n_s
    tt = min(ROUTE_ROWS, n_s)
    assert n_p % tt == 0 and n_s % tt == 0
    eid, gate, rank, cnt = _router(xp, xs, mp['g'], mp['wth'], mp['wtl'], mp['bias'], tt=tt)
    bm = EXPERT_ROWS
    counts = cnt[:, 0].astype(jnp.int32)
    padded = (counts + bm - 1) // bm * bm
    pend = jnp.cumsum(padded)
    pstart = pend - padded
    sel = eid[:, :, None] == jnp.arange(N_EXPERTS, dtype=jnp.int32)
    dest = rank + jnp.sum(jnp.where(sel, pstart, 0), axis=-1)
    dest = dest.T.reshape(-1)
    p_rows = (2 * n + N_EXPERTS * (bm - 1)) // bm * bm
    nblk = p_rows // bm
    block_start = jnp.arange(nblk, dtype=jnp.int32) * bm
    n_used = (pend[-1] // bm).astype(jnp.int32)
    block_e = jnp.sum((pend[None, :] <= block_start[:, None]).astype(jnp.int32), axis=1)
    last_e = jnp.sum((pend <= pend[-1] - 1).astype(jnp.int32))
    block_e = jnp.minimum(block_e, last_e).astype(jnp.int32)
    n_used = n_used.reshape(1)
    buf = _dispatch((pend - bm).astype(jnp.int32), counts, n_used, dest, xp, xs, p_rows, tt=tt, bm=bm)
    yb = _experts(block_e, n_used, buf, mp['g'], mp['w1'], mp['w3'], mp['w2'], bm=bm)
    gate_t = gate.T
    y_p = _combine(dest, xp, gate_t, g_final, yb, tt=tt, row_off=0)
    y_s = _combine(dest, xs, gate_t, g_final, yb, tt=tt, row_off=n_p)
    return y_p, y_s


def _block_diag_pack(w, per):
    h, hd, _ = w.shape
    wg = w.reshape(h // per, per, hd, hd)
    eye = jnp.eye(per, dtype=w.dtype)
    return jnp.einsum('gpij,pq->gpiqj', wg, eye).reshape(h // per, per * hd, per * hd)


def kernel(x_prompt, x_sample, mem_prompt, state_conv_a, state_rglru_h, cache_mem_k, cache_mem_v, norm_mix, w_in, conv_a_w, conv_a_b, lru_w_r, lru_b_r, lru_w_i, lru_b_i, lru_lambda, v_norm_g, mlp_w_s, mlp_b_s, out_norm_a, out_norm_b, w_out, norm_mem_q, norm_mem_kv, w_mem_q, w_mem_k, w_mem_v, w_mem_o, norm_ffn, w_router_c, b_router_c, w_router_f, b_router_f, w_exp_1, w_exp_3, w_exp_2, norm_final):
    depth = w_in.shape[0]
    assert depth == 1
    l = 0
    bp, tp, d = x_prompt.shape
    bs, ts, _ = x_sample.shape
    d_a = conv_a_b.shape[-1]
    a_hd = lru_w_r.shape[-1]
    per = V7X_MXU_DIM // a_hd
    row = lambda v: v.reshape(1, -1)

    mix_p = dict(
        gin=row(norm_mix[l]), win=w_in[l].astype(BF16), cw=conv_a_w[l], cb=row(conv_a_b[l]),
        wr=_block_diag_pack(lru_w_r[l], per).astype(BF16), br=row(lru_b_r[l]),
        wi=_block_diag_pack(lru_w_i[l], per).astype(BF16), bi=row(lru_b_i[l]),
        lam=row(lru_lambda[l]), vg=row(v_norm_g[l]), ws=mlp_w_s[l], bst=mlp_b_s[l].T,
        goa=row(out_norm_a[l]), gob=row(out_norm_b[l]), wout=w_out[l].astype(BF16))

    conv0 = jnp.zeros((bp, CONV_W - 1, d_a), F32)
    h0 = jnp.zeros((bp, 1, d_a), F32)
    tt_p = min(MIX_ROWS, tp)
    x1p, conv_p, h_p = _mixer(x_prompt, conv0, h0, mix_p, bt=1, tt=tt_p, pos0=0, emit_vn=False)
    bt_mix = 8 if bs % 8 == 0 else 1
    x1s, conv_s, h_s, vn_s = _mixer(x_sample, state_conv_a[l], state_rglru_h[l][:, None, :], mix_p,
                                    bt=bt_mix, tt=ts, pos0=PAST_LEN, emit_vn=True)

    mt = mem_prompt.shape[1]
    k_p, v_p, kb_p, vb_p = _mem_kv(mem_prompt.reshape(bp * mt, d), row(norm_mem_kv[l]),
                                   w_mem_k[l].astype(BF16), w_mem_v[l].astype(BF16))
    gq = row(norm_mem_q[l])
    wq = w_mem_q[l].astype(BF16)
    wo = w_mem_o[l].astype(BF16)
    x2p = _attend(x1p, kb_p.reshape(bp, mt, d), vb_p.reshape(bp, mt, d), gq, wq, wo, bt=1, tt=tt_p)
    bt_s = 4 if bs % 4 == 0 else 1
    x2s = _attend(x1s, cache_mem_k[l].astype(BF16).reshape(bs, mt, d),
                  cache_mem_v[l].astype(BF16).reshape(bs, mt, d), gq, wq, wo, bt=bt_s, tt=ts)

    wt = jnp.concatenate([w_router_c[l].T, jnp.zeros((8 - N_GROUPS, d), F32), w_router_f[l].T], axis=0)
    wth = wt.astype(BF16)
    wtl = (wt - wth.astype(F32)).astype(BF16)
    bias = jnp.concatenate([b_router_c[l], jnp.zeros((8 - N_GROUPS,), F32), b_router_f[l]]).reshape(-1, 1)
    moe_p = dict(g=row(norm_ffn[l]), wth=wth, wtl=wtl, bias=bias, w1=w_exp_1[l], w3=w_exp_3[l], w2=w_exp_2[l])
    y_p, y_s = _moe_final(x2p, x2s, moe_p, row(norm_final))

    heads_b = v_norm_g.shape[1]
    hd_m = d // MEM_HEADS
    return (y_p.reshape(bp, tp, d), y_s.reshape(bs, ts, d),
            conv_p[None], h_p.reshape(1, bp, d_a),
            k_p.reshape(1, bp, mt, MEM_HEADS, hd_m), v_p.reshape(1, bp, mt, MEM_HEADS, hd_m),
            conv_s[None], h_s.reshape(1, bs, d_a),
            vn_s.reshape(1, bs, ts, heads_b, B_HD))
```

```python
import functools

import jax
import jax.numpy as jnp
from jax import lax
from jax.experimental import pallas as pl
from jax.experimental.pallas import tpu as pltpu

EPS = 1e-6
LRU_C = 8.0
CONV_W = 4
A_HEADS = 16
B_HD = 128
MLP_CHUNK = 128
MEM_HEADS = 4
N_GROUPS = 4
EXPERTS_PER_GROUP = 8
N_EXPERTS = N_GROUPS * EXPERTS_PER_GROUP
PAST_LEN = 1024

BF16 = jnp.bfloat16
F32 = jnp.float32

V7X_VMEM_BYTES = 64 * 1024 * 1024
V7X_MXU_DIM = 256
SUBLANES = 8
XA_ROW0 = SUBLANES
FINE_ROW0 = SUBLANES

MIX_ROWS = 256
ROUTE_ROWS = 512
EXPERT_ROWS = 512
ZERO_ROWS = 128
VMEM_LIMIT = V7X_VMEM_BYTES * 7 // 8


def _cparams(sem):
    return pltpu.CompilerParams(dimension_semantics=sem, vmem_limit_bytes=VMEM_LIMIT)


def _rms(x, g):
    return x * lax.rsqrt(jnp.mean(x * x, axis=-1, keepdims=True) + EPS) * g


def _dot(a, b):
    return jnp.dot(a, b, preferred_element_type=F32)


def _dot_nt(a, b):
    return lax.dot_general(a, b, (((1,), (1,)), ((), ())), preferred_element_type=F32)


def _full(shape):
    nd = len(shape)
    return pl.BlockSpec(shape, lambda *_: (0,) * nd)


def _gelu_proj(xn, w_ref, lo, width):
    cc = V7X_MXU_DIM
    return jnp.concatenate([jax.nn.gelu(_dot(xn, w_ref[:, lo + c * cc:lo + (c + 1) * cc]))
                            for c in range(width // cc)], axis=-1)


def _mixer_kernel(x_ref, conv0_ref, h0_ref, gin_ref, win_ref, cw_ref, cb_ref, wr_ref, br_ref,
                  wi_ref, bi_ref, lam_ref, vg_ref, ws_ref, bst_ref, goa_ref, gob_ref, wout_ref,
                  *rest, bt, tt, chunk, pos0, emit_vn):
    if emit_vn:
        y_ref, convo_ref, ho_ref, vn_ref = rest[:4]
        scratch = rest[4:]
    else:
        y_ref, convo_ref, ho_ref = rest[:3]
        vn_ref = None
        scratch = rest[3:]
    xa_buf, a_buf, u_buf, h_buf, z_buf, hc_ref, ycat = scratch
    t = pl.program_id(1)
    d = x_ref.shape[-1]
    d_a = a_buf.shape[-1]
    d_b = z_buf.shape[-1]
    m = bt * tt

    hist = XA_ROW0 - (CONV_W - 1)

    @pl.when(t == 0)
    def _():
        xa_buf[:, hist:XA_ROW0, :] = conv0_ref[...]
        hc_ref[...] = h0_ref[...]

    x = x_ref[...].reshape(m, d)
    xn = _rms(x, gin_ref[...]).astype(BF16)
    xa = _dot(xn, win_ref[:, 0:d_a])
    gga = _gelu_proj(xn, win_ref, d_a, d_a)
    gub = _gelu_proj(xn, win_ref, 2 * d_a, d_b)
    gv = _gelu_proj(xn, win_ref, 2 * d_a + d_b, d_b)

    xa_buf[:, XA_ROW0:XA_ROW0 + tt, :] = xa.reshape(bt, tt, d_a)
    xc = cb_ref[...][None]
    for k in range(CONV_W):
        xc = xc + xa_buf[:, hist + k:hist + k + tt, :] * cw_ref[k:k + 1, :][None]
    tail = xa_buf[:, tt + hist:tt + XA_ROW0, :]
    convo_ref[...] = tail
    xa_buf[:, hist:XA_ROW0, :] = tail
    xc = xc.reshape(m, d_a)

    xcb = xc.astype(BF16)
    gw = V7X_MXU_DIM
    r_pre = jnp.concatenate([_dot(xcb[:, g * gw:(g + 1) * gw], wr_ref[g]) for g in range(d_a // gw)], axis=-1)
    i_pre = jnp.concatenate([_dot(xcb[:, g * gw:(g + 1) * gw], wi_ref[g]) for g in range(d_a // gw)], axis=-1)
    r = jax.nn.sigmoid(r_pre + br_ref[...])
    ig = jax.nn.sigmoid(i_pre + bi_ref[...])
    nl = -lam_ref[...]
    softplus = jnp.maximum(nl, 0.0) + jnp.log1p(jnp.exp(-jnp.abs(nl)))
    log_a = -LRU_C * r * softplus
    a = jnp.exp(log_a)
    m2 = -jnp.tanh(log_a) * (1.0 + a * a)
    mult = jnp.where(m2 > 0.0, m2 * lax.rsqrt(m2), 0.0)
    if pos0 == 0:
        row = lax.broadcasted_iota(jnp.int32, (m, d_a), 0)
        is_first = jnp.logical_and(t == 0, (row % tt) == 0)
        mult = jnp.where(is_first, 1.0, mult)
    u = mult * (ig * xc)
    a_buf[...] = a.reshape(bt, tt, d_a)
    u_buf[...] = u.reshape(bt, tt, d_a)

    sub = lax.broadcasted_iota(jnp.int32, (SUBLANES, d_a), 0)
    for b in range(bt):
        h = hc_ref[b]
        for j in range(tt // SUBLANES):
            r0 = j * SUBLANES
            av = a_buf[b, r0:r0 + SUBLANES, :]
            uv = u_buf[b, r0:r0 + SUBLANES, :]
            for s in (1, 2, 4):
                a_sh = jnp.where(sub >= s, pltpu.roll(av, s, 0), 1.0)
                u_sh = jnp.where(sub >= s, pltpu.roll(uv, s, 0), 0.0)
                uv = av * u_sh + uv
                av = av * a_sh
            hv = av * h + uv
            h_buf[b, r0:r0 + SUBLANES, :] = hv
            h = hv[SUBLANES - 1:SUBLANES, :]
        hc_ref[b] = h
    ho_ref[...] = hc_ref[...]

    ya = h_buf[...].reshape(m, d_a) * gga
    ycat[:, 0:d_a] = _rms(ya, goa_ref[...]).astype(BF16)

    heads = d_b // B_HD
    ri = lax.broadcasted_iota(jnp.int32, (chunk, chunk), 0)
    ci = lax.broadcasted_iota(jnp.int32, (chunk, chunk), 1)
    for hh in range(heads):
        lo = hh * B_HD
        gvh = gv[:, lo:lo + B_HD]
        vn_h = gvh * lax.rsqrt(jnp.mean(gvh * gvh, axis=-1, keepdims=True) + EPS) * vg_ref[:, lo:lo + B_HD]
        if emit_vn:
            vn_ref[:, :, lo:lo + B_HD] = vn_h.reshape(bt, tt, B_HD)
        vn_hb = vn_h.astype(BF16)
        ws_h = jnp.where(ci <= ri, ws_ref[hh], 0.0).astype(BF16)
        bias = bst_ref[:, hh:hh + 1]
        for c in range(m // chunk):
            z = _dot(ws_h, vn_hb[c * chunk:(c + 1) * chunk, :]) + bias
            z_buf[c * chunk:(c + 1) * chunk, lo:lo + B_HD] = z
    yb = gub * z_buf[...]
    ycat[:, d_a:] = _rms(yb, gob_ref[...]).astype(BF16)

    y_ref[...] = (x + _dot(ycat[...], wout_ref[...])).reshape(bt, tt, d)


def _mixer(x, conv0, h0, p, *, bt, tt, pos0, emit_vn):
    nb, seq, d = x.shape
    d_a = conv0.shape[-1]
    d_b = p['vg'].shape[-1]
    chunk = min(tt, MLP_CHUNK)
    assert nb % bt == 0 and seq % tt == 0 and tt % chunk == 0 and tt % SUBLANES == 0
    grid = (nb // bt, seq // tt)
    weights = [p['gin'], p['win'], p['cw'], p['cb'], p['wr'], p['br'], p['wi'], p['bi'], p['lam'],
               p['vg'], p['ws'][:, :chunk, :chunk], p['bst'][:chunk], p['goa'], p['gob'], p['wout']]
    in_specs = [pl.BlockSpec((bt, tt, d), lambda b, t: (b, t, 0)),
                pl.BlockSpec((bt, CONV_W - 1, d_a), lambda b, t: (b, 0, 0)),
                pl.BlockSpec((bt, 1, d_a), lambda b, t: (b, 0, 0))]
    in_specs += [_full(w.shape) for w in weights]
    out_shape = [jax.ShapeDtypeStruct((nb, seq, d), F32),
                 jax.ShapeDtypeStruct((nb, CONV_W - 1, d_a), F32),
                 jax.ShapeDtypeStruct((nb, 1, d_a), F32)]
    out_specs = [pl.BlockSpec((bt, tt, d), lambda b, t: (b, t, 0)),
                 pl.BlockSpec((bt, CONV_W - 1, d_a), lambda b, t: (b, 0, 0)),
                 pl.BlockSpec((bt, 1, d_a), lambda b, t: (b, 0, 0))]
    if emit_vn:
        out_shape.append(jax.ShapeDtypeStruct((nb, seq, d_b), F32))
        out_specs.append(pl.BlockSpec((bt, tt, d_b), lambda b, t: (b, t, 0)))
    scratch = [pltpu.VMEM((bt, tt + XA_ROW0, d_a), F32), pltpu.VMEM((bt, tt, d_a), F32),
               pltpu.VMEM((bt, tt, d_a), F32), pltpu.VMEM((bt, tt, d_a), F32),
               pltpu.VMEM((bt * tt, d_b), F32), pltpu.VMEM((bt, 1, d_a), F32),
               pltpu.VMEM((bt * tt, d), BF16)]
    kern = functools.partial(_mixer_kernel, bt=bt, tt=tt, chunk=chunk, pos0=pos0, emit_vn=emit_vn)
    return pl.pallas_call(
        kern, grid=grid, in_specs=in_specs, out_specs=out_specs, out_shape=out_shape,
        scratch_shapes=scratch, compiler_params=_cparams(("arbitrary", "arbitrary")),
        name="mixer")(x, conv0, h0, *weights)


def _memkv_kernel(m_ref, g_ref, wk_ref, wv_ref, k_ref, v_ref, kb_ref, vb_ref):
    mn = _rms(m_ref[...], g_ref[...]).astype(BF16)
    k = _dot(mn, wk_ref[...])
    v = _dot(mn, wv_ref[...])
    hd = k_ref.shape[-1]
    for hh in range(k_ref.shape[1]):
        k_ref[:, hh, :] = k[:, hh * hd:(hh + 1) * hd]
        v_ref[:, hh, :] = v[:, hh * hd:(hh + 1) * hd]
    kb_ref[...] = k.astype(BF16)
    vb_ref[...] = v.astype(BF16)


def _mem_kv(mem2d, g, wk, wv, *, tm=256):
    n, d = mem2d.shape
    hd = d // MEM_HEADS
    f_spec = pl.BlockSpec((tm, MEM_HEADS, hd), lambda i: (i, 0, 0))
    b_spec = pl.BlockSpec((tm, d), lambda i: (i, 0))
    return pl.pallas_call(
        _memkv_kernel, grid=(n // tm,),
        in_specs=[pl.BlockSpec((tm, d), lambda i: (i, 0)), _full(g.shape), _full(wk.shape), _full(wv.shape)],
        out_specs=[f_spec, f_spec, b_spec, b_spec],
        out_shape=[jax.ShapeDtypeStruct((n, MEM_HEADS, hd), F32), jax.ShapeDtypeStruct((n, MEM_HEADS, hd), F32),
                   jax.ShapeDtypeStruct((n, d), BF16), jax.ShapeDtypeStruct((n, d), BF16)],
        compiler_params=_cparams(("arbitrary",)), name="mem_kv")(mem2d, g, wk, wv)


def _attend_kernel(x_ref, k_ref, v_ref, g_ref, wq_ref, wo_ref, y_ref, o_buf, *, bt, tt):
    d = x_ref.shape[-1]
    hd = d // MEM_HEADS
    x = x_ref[...]
    xn = _rms(x, g_ref[...]).astype(BF16)
    q = _dot(xn, wq_ref[...]).astype(BF16)
    scale = hd ** -0.5
    for b in range(bt):
        for hh in range(MEM_HEADS):
            qh = q[b * tt:(b + 1) * tt, hh * hd:(hh + 1) * hd]
            s = _dot_nt(qh, k_ref[b, :, hh * hd:(hh + 1) * hd]) * scale
            e = jnp.exp(s - jnp.max(s, axis=-1, keepdims=True))
            pr = e / jnp.sum(e, axis=-1, keepdims=True)
            o = _dot(pr.astype(BF16), v_ref[b, :, hh * hd:(hh + 1) * hd])
            o_buf[b * tt:(b + 1) * tt, hh * hd:(hh + 1) * hd] = o.astype(BF16)
    y_ref[...] = x + _dot(o_buf[...], wo_ref[...])


def _attend(x, k, v, g, wq, wo, *, bt, tt):
    nb, seq, d = x.shape
    mt = k.shape[1]
    m = bt * tt
    n = nb * seq
    assert n % m == 0 and (bt == 1 or tt == seq)
    kern = functools.partial(_attend_kernel, bt=bt, tt=tt)
    kv_spec = pl.BlockSpec((bt, mt, d), lambda i: ((i * m) // (bt * seq), 0, 0))
    return pl.pallas_call(
        kern, grid=(n // m,),
        in_specs=[pl.BlockSpec((m, d), lambda i: (i, 0)), kv_spec, kv_spec,
                  _full(g.shape), _full(wq.shape), _full(wo.shape)],
        out_specs=pl.BlockSpec((m, d), lambda i: (i, 0)),
        out_shape=jax.ShapeDtypeStruct((n, d), F32),
        scratch_shapes=[pltpu.VMEM((m, d), BF16)],
        compiler_params=_cparams(("arbitrary",)), name="attend")(x.reshape(n, d), k, v, g, wq, wo)


def _two_source_specs(tt, d, npb):
    return [pl.BlockSpec((tt, d), lambda i, *_: (jnp.minimum(i, npb - 1), 0)),
            pl.BlockSpec((tt, d), lambda i, *_: (jnp.maximum(i - npb, 0), 0))]


def _router_kernel(xp_ref, xs_ref, g_ref, wth_ref, wtl_ref, bias_ref, eid_ref, gate_ref, rank_ref, cnt_ref, *, npb):
    i = pl.program_id(0)
    tt = xp_ref.shape[0]

    @pl.when(i == 0)
    def _():
        cnt_ref[...] = jnp.zeros_like(cnt_ref)

    x = jnp.where(i < npb, xp_ref[...], xs_ref[...])
    xn = _rms(x, g_ref[...])
    xh = xn.astype(BF16)
    xl = (xn - xh.astype(F32)).astype(BF16)
    logits = (_dot_nt(wth_ref[...], xh) + _dot_nt(wth_ref[...], xl) + _dot_nt(wtl_ref[...], xh)) + bias_ref[...]
    lc = logits[0:N_GROUPS]
    ec = jnp.exp(lc - jnp.max(lc, axis=0, keepdims=True))
    pc = ec / jnp.sum(ec, axis=0, keepdims=True)
    p_grp = jnp.max(pc, axis=0, keepdims=True)
    gi = lax.broadcasted_iota(jnp.int32, pc.shape, 0)
    grp = jnp.min(jnp.where(pc == p_grp, gi, N_GROUPS), axis=0, keepdims=True)
    lf = jnp.zeros((EXPERTS_PER_GROUP, tt), F32)
    for gidx in range(N_GROUPS):
        lo = FINE_ROW0 + gidx * EXPERTS_PER_GROUP
        lf = lf + jnp.where(grp == gidx, logits[lo:lo + EXPERTS_PER_GROUP], 0.0)
    fi = lax.broadcasted_iota(jnp.int32, lf.shape, 0)
    t1 = jnp.max(lf, axis=0, keepdims=True)
    i1 = jnp.min(jnp.where(lf == t1, fi, EXPERTS_PER_GROUP), axis=0, keepdims=True)
    lf2 = jnp.where(fi == i1, -jnp.inf, lf)
    t2 = jnp.max(lf2, axis=0, keepdims=True)
    i2 = jnp.min(jnp.where(lf2 == t2, fi, EXPERTS_PER_GROUP), axis=0, keepdims=True)
    e2 = jnp.exp(t2 - t1)
    den = 1.0 + e2
    gate_ref[0:1, :] = p_grp * (1.0 / den)
    gate_ref[1:2, :] = p_grp * (e2 / den)
    eid1 = grp * EXPERTS_PER_GROUP + i1
    eid2 = grp * EXPERTS_PER_GROUP + i2
    eid_ref[0:1, :] = eid1
    eid_ref[1:2, :] = eid2

    ei = lax.broadcasted_iota(jnp.int32, (N_EXPERTS, tt), 0)
    oh1 = ei == eid1
    oh2 = ei == eid2
    oh = jnp.where(jnp.logical_or(oh1, oh2), 1.0, 0.0)
    tr = lax.broadcasted_iota(jnp.int32, (tt, tt), 0)
    tc = lax.broadcasted_iota(jnp.int32, (tt, tt), 1)
    upper = jnp.where(tr < tc, 1.0, 0.0).astype(BF16)
    run = cnt_ref[:, 0:1]
    before = _dot(oh.astype(BF16), upper) + run
    rank_ref[0:1, :] = jnp.sum(jnp.where(oh1, before, 0.0), axis=0, keepdims=True).astype(jnp.int32)
    rank_ref[1:2, :] = jnp.sum(jnp.where(oh2, before, 0.0), axis=0, keepdims=True).astype(jnp.int32)
    cnt_ref[...] = cnt_ref[...] + jnp.sum(oh, axis=1, keepdims=True)


def _router(xp, xs, g, wth, wtl, bias, *, tt):
    (n_p, d), n_s = xp.shape, xs.shape[0]
    npb, n = n_p // tt, n_p + n_s
    tok = pl.BlockSpec((2, tt), lambda i: (0, i))
    return pl.pallas_call(
        functools.partial(_router_kernel, npb=npb), grid=(n // tt,),
        in_specs=_two_source_specs(tt, d, npb) + [_full(g.shape), _full(wth.shape), _full(wtl.shape),
                                                  _full(bias.shape)],
        out_specs=[tok, tok, tok, pl.BlockSpec((N_EXPERTS, 128), lambda i: (0, 0))],
        out_shape=[jax.ShapeDtypeStruct((2, n), jnp.int32),
                   jax.ShapeDtypeStruct((2, n), F32),
                   jax.ShapeDtypeStruct((2, n), jnp.int32),
                   jax.ShapeDtypeStruct((N_EXPERTS, 128), F32)],
        compiler_params=_cparams(("arbitrary",)), name="router")(xp, xs, g, wth, wtl, bias)


def _dispatch_kernel(zlo_ref, zhi_ref, dest_ref, xp_hbm, xs_hbm, buf_ref, zero_buf, xbuf, fsem, ssem, zsem,
                     *, npb, nt):
    i = pl.program_id(0)
    tt = xbuf.shape[1]

    def fetch(t, slot):
        @pl.when(t < npb)
        def _():
            pltpu.make_async_copy(xp_hbm.at[pl.ds(pl.multiple_of(t * tt, tt), tt), :],
                                  xbuf.at[slot], fsem.at[slot]).start()

        @pl.when(t >= npb)
        def _():
            pltpu.make_async_copy(xs_hbm.at[pl.ds(pl.multiple_of((t - npb) * tt, tt), tt), :],
                                  xbuf.at[slot], fsem.at[slot]).start()

    def fetch_wait(slot):
        pltpu.make_async_copy(xp_hbm.at[pl.ds(0, tt), :], xbuf.at[slot], fsem.at[slot]).wait()

    def scatter_wait(slot):
        for k in range(2):
            pltpu.make_async_copy(xbuf.at[slot], buf_ref.at[pl.ds(0, tt), :], ssem.at[slot]).wait()

    @pl.when(i == 0)
    def _():
        zero_buf[...] = jnp.zeros_like(zero_buf)
        zc = zero_buf.shape[0]

        def zcopy(chunk):
            return pltpu.make_async_copy(zero_buf, buf_ref.at[pl.ds(pl.multiple_of(chunk * zc, zc), zc), :], zsem)

        def zstart(chunk, c):
            zcopy(chunk).start()
            return c

        def zwait(chunk, c):
            zcopy(chunk).wait()
            return c

        for e in range(zlo_ref.shape[0]):
            lax.fori_loop(zlo_ref[e], zhi_ref[e], zstart, 0)
        for e in range(zlo_ref.shape[0]):
            lax.fori_loop(zlo_ref[e], zhi_ref[e], zwait, 0)
        fetch(0, 0)

    slot = i % 3
    nxt = (i + 1) % 3

    @pl.when(i >= 2)
    def _():
        scatter_wait(nxt)

    @pl.when(i + 1 < nt)
    def _():
        fetch(i + 1, nxt)

    fetch_wait(slot)
    x_ref = xbuf.at[slot]
    s_sem = ssem.at[slot]

    def group(gidx, c):
        r0 = pl.multiple_of(gidx * SUBLANES, SUBLANES)
        rows = x_ref.at[pl.ds(r0, SUBLANES), :]
        for j in range(SUBLANES):
            for k in range(2):
                pltpu.make_async_copy(rows.at[pl.ds(j, 1), :],
                                      buf_ref.at[pl.ds(dest_ref[2 * r0 + (2 * j + k)], 1), :], s_sem).start()
        return c

    lax.fori_loop(0, tt // SUBLANES, group, 0)

    @pl.when(i == nt - 1)
    def _():
        if nt > 1:
            scatter_wait((i + 2) % 3)
        scatter_wait(slot)


def _dispatch(pad_lo, pad_hi, dest, xp, xs, p_rows, *, tt):
    (n_p, d), n_s = xp.shape, xs.shape[0]
    npb, n = n_p // tt, n_p + n_s
    nt = n // tt
    zlo = (pad_lo // ZERO_ROWS).astype(jnp.int32)
    zhi = (pad_hi // ZERO_ROWS).astype(jnp.int32)
    gs = pltpu.PrefetchScalarGridSpec(
        num_scalar_prefetch=2, grid=(nt,),
        in_specs=[pl.BlockSpec((2 * tt,), lambda i, *_: (i,), memory_space=pltpu.SMEM),
                  pl.BlockSpec(memory_space=pl.ANY), pl.BlockSpec(memory_space=pl.ANY)],
        out_specs=pl.BlockSpec(memory_space=pl.ANY),
        scratch_shapes=[pltpu.VMEM((ZERO_ROWS, d), F32), pltpu.VMEM((3, tt, d), F32),
                        pltpu.SemaphoreType.DMA((3,)), pltpu.SemaphoreType.DMA((3,)),
                        pltpu.SemaphoreType.DMA(())])
    return pl.pallas_call(
        functools.partial(_dispatch_kernel, npb=npb, nt=nt), grid_spec=gs,
        out_shape=jax.ShapeDtypeStruct((p_rows, d), F32),
        compiler_params=_cparams(("arbitrary",)), name="dispatch")(zlo, zhi, dest, xp, xs)


def _experts_kernel(be_ref, nu_ref, x_ref, g_ref, w1_ref, w3_ref, w2_ref, o_ref, w1b, w3b, w2b):
    i = pl.program_id(0)

    @pl.when(i < nu_ref[0])
    def _():
        prev = be_ref[jnp.maximum(i - 1, 0)]

        @pl.when(jnp.logical_or(i == 0, be_ref[i] != prev))
        def _():
            w1b[...] = w1_ref[0].astype(BF16)
            w3b[...] = w3_ref[0].astype(BF16)
            w2b[...] = w2_ref[0].astype(BF16)

        x = x_ref[...]
        xg = (x * g_ref[...]).astype(BF16)
        scale = lax.rsqrt(jnp.mean(x * x, axis=-1, keepdims=True) + EPS)
        h1 = _dot(xg, w1b[...]) * scale
        h3 = _dot(xg, w3b[...]) * scale
        hd = (h1 * jax.nn.sigmoid(h1) * h3).astype(BF16)
        o_ref[...] = _dot(hd, w2b[...])

    @pl.when(i >= nu_ref[0])
    def _():
        o_ref[...] = jnp.zeros_like(o_ref)


def _experts(block_e, n_used, buf, g, w1, w3, w2, *, bm):
    p_rows, d = buf.shape
    ne, _, de = w1.shape
    nblk = p_rows // bm
    gs = pltpu.PrefetchScalarGridSpec(
        num_scalar_prefetch=2, grid=(nblk,),
        in_specs=[pl.BlockSpec((bm, d), lambda i, be, nu: (jnp.minimum(i, nu[0] - 1), 0)),
                  pl.BlockSpec(g.shape, lambda i, be, nu: (0, 0)),
                  pl.BlockSpec((1, d, de), lambda i, be, nu: (be[i], 0, 0)),
                  pl.BlockSpec((1, d, de), lambda i, be, nu: (be[i], 0, 0)),
                  pl.BlockSpec((1, de, d), lambda i, be, nu: (be[i], 0, 0))],
        out_specs=pl.BlockSpec((bm, d), lambda i, be, nu: (i, 0)),
        scratch_shapes=[pltpu.VMEM((d, de), BF16), pltpu.VMEM((d, de), BF16), pltpu.VMEM((de, d), BF16)])
    return pl.pallas_call(
        _experts_kernel, grid_spec=gs,
        out_shape=jax.ShapeDtypeStruct((p_rows, d), F32),
        compiler_params=_cparams(("arbitrary",)), name="experts")(block_e, n_used, buf, g, w1, w3, w2)


def _combine_kernel(dest_ref, x_ref, gate_ref, g_ref, yb_ref, o_ref, a0, b0, a1, b1, sems, *, last):
    i = pl.program_id(0)
    tt = x_ref.shape[0]
    slots = ((a0, b0), (a1, b1))

    def issue(slot):
        for r in range(tt):
            for k in range(2):
                pltpu.make_async_copy(yb_ref.at[pl.ds(dest_ref[2 * r + k], 1), :],
                                      slots[slot][k].at[pl.ds(r, 1), :], sems.at[slot]).start()

    def wait(slot):
        for k in range(2):
            pltpu.make_async_copy(yb_ref.at[pl.ds(0, tt), :], slots[slot][k], sems.at[slot]).wait()

    @pl.when(i == 0)
    def _():
        issue(0)

    for par in range(2):
        @pl.when(jnp.logical_and(i >= 1, i % 2 == par))
        def _(par=par):
            cur = 1 - par
            wait(cur)
            issue(par)
            moe = gate_ref[:, 0:1] * slots[cur][0][...] + gate_ref[:, 1:2] * slots[cur][1][...]
            o_ref[...] = _rms(x_ref[...] + moe, g_ref[...])

            if par == last % 2:
                @pl.when(i == last)
                def _():
                    wait(par)


def _combine(dest_flat, x2d, gate_t, g, yb, *, tt, row_off):
    n, d = x2d.shape
    off = row_off // tt
    nt = n // tt
    prev = lambda i: jnp.maximum(i - 1, 0)
    return pl.pallas_call(
        functools.partial(_combine_kernel, last=nt), grid=(nt + 1,),
        in_specs=[pl.BlockSpec((2 * tt,), lambda i: (jnp.minimum(i, nt - 1) + off,), memory_space=pltpu.SMEM),
                  pl.BlockSpec((tt, d), lambda i: (prev(i), 0)),
                  pl.BlockSpec((tt, 2), lambda i: (prev(i) + off, 0)),
                  _full(g.shape),
                  pl.BlockSpec(memory_space=pl.ANY)],
        out_specs=pl.BlockSpec((tt, d), lambda i: (prev(i), 0)),
        out_shape=jax.ShapeDtypeStruct((n, d), F32),
        scratch_shapes=[pltpu.VMEM((tt, d), F32)] * 4 + [pltpu.SemaphoreType.DMA((2,))],
        compiler_params=_cparams(("arbitrary",)), name="combine")(dest_flat, x2d, gate_t, g, yb)


def _moe_final(xp, xs, mp, g_final):
    (n_p, d), n_s = xp.shape, xs.shape[0]
    n = n_p + n_s
    tt = min(ROUTE_ROWS, n_s)
    assert n_p % tt == 0 and n_s % tt == 0
    eid, gate, rank, cnt = _router(xp, xs, mp['g'], mp['wth'], mp['wtl'], mp['bias'], tt=tt)
    bm = EXPERT_ROWS
    counts = cnt[:, 0].astype(jnp.int32)
    padded = (counts + bm - 1) // bm * bm
    pend = jnp.cumsum(padded)
    pstart = pend - padded
    sel = eid[:, :, None] == jnp.arange(N_EXPERTS, dtype=jnp.int32)
    dest = rank + jnp.sum(jnp.where(sel, pstart, 0), axis=-1)
    dest = dest.T.reshape(-1)
    p_rows = (2 * n + N_EXPERTS * (bm - 1)) // bm * bm
    nblk = p_rows // bm
    block_start = jnp.arange(nblk, dtype=jnp.int32) * bm
    n_used = (pend[-1] // bm).astype(jnp.int32)
    block_e = jnp.sum((pend[None, :] <= block_start[:, None]).astype(jnp.int32), axis=1)
    last_e = jnp.sum((pend <= pend[-1] - 1).astype(jnp.int32))
    block_e = jnp.minimum(block_e, last_e).astype(jnp.int32)
    n_used = n_used.reshape(1)
    pad_lo = jnp.concatenate([pstart + counts, pend[-1:]])
    pad_hi = jnp.concatenate([pend, jnp.full((1,), p_rows, jnp.int32)])
    buf = _dispatch(pad_lo, pad_hi, dest, xp, xs, p_rows, tt=tt)
    yb = _experts(block_e, n_used, buf, mp['g'], mp['w1'], mp['w3'], mp['w2'], bm=bm)
    gate_t = gate.T
    y_p = _combine(dest, xp, gate_t, g_final, yb, tt=tt, row_off=0)
    y_s = _combine(dest, xs, gate_t, g_final, yb, tt=tt, row_off=n_p)
    return y_p, y_s


def _block_diag_pack(w, per):
    h, hd, _ = w.shape
    wg = w.reshape(h // per, per, hd, hd)
    eye = jnp.eye(per, dtype=w.dtype)
    return jnp.einsum('gpij,pq->gpiqj', wg, eye).reshape(h // per, per * hd, per * hd)


def kernel(x_prompt, x_sample, mem_prompt, state_conv_a, state_rglru_h, cache_mem_k, cache_mem_v, norm_mix, w_in, conv_a_w, conv_a_b, lru_w_r, lru_b_r, lru_w_i, lru_b_i, lru_lambda, v_norm_g, mlp_w_s, mlp_b_s, out_norm_a, out_norm_b, w_out, norm_mem_q, norm_mem_kv, w_mem_q, w_mem_k, w_mem_v, w_mem_o, norm_ffn, w_router_c, b_router_c, w_router_f, b_router_f, w_exp_1, w_exp_3, w_exp_2, norm_final):
    depth = w_in.shape[0]
    assert depth == 1
    l = 0
    bp, tp, d = x_prompt.shape
    bs, ts, _ = x_sample.shape
    d_a = conv_a_b.shape[-1]
    a_hd = lru_w_r.shape[-1]
    per = V7X_MXU_DIM // a_hd
    row = lambda v: v.reshape(1, -1)

    mix_p = dict(
        gin=row(norm_mix[l]), win=w_in[l].astype(BF16), cw=conv_a_w[l], cb=row(conv_a_b[l]),
        wr=_block_diag_pack(lru_w_r[l], per).astype(BF16), br=row(lru_b_r[l]),
        wi=_block_diag_pack(lru_w_i[l], per).astype(BF16), bi=row(lru_b_i[l]),
        lam=row(lru_lambda[l]), vg=row(v_norm_g[l]), ws=mlp_w_s[l], bst=mlp_b_s[l].T,
        goa=row(out_norm_a[l]), gob=row(out_norm_b[l]), wout=w_out[l].astype(BF16))

    conv0 = jnp.zeros((bp, CONV_W - 1, d_a), F32)
    h0 = jnp.zeros((bp, 1, d_a), F32)
    tt_p = min(MIX_ROWS, tp)
    x1p, conv_p, h_p = _mixer(x_prompt, conv0, h0, mix_p, bt=1, tt=tt_p, pos0=0, emit_vn=False)
    bt_mix = 8 if bs % 8 == 0 else 1
    x1s, conv_s, h_s, vn_s = _mixer(x_sample, state_conv_a[l], state_rglru_h[l][:, None, :], mix_p,
                                    bt=bt_mix, tt=ts, pos0=PAST_LEN, emit_vn=True)

    mt = mem_prompt.shape[1]
    k_p, v_p, kb_p, vb_p = _mem_kv(mem_prompt.reshape(bp * mt, d), row(norm_mem_kv[l]),
                                   w_mem_k[l].astype(BF16), w_mem_v[l].astype(BF16))
    gq = row(norm_mem_q[l])
    wq = w_mem_q[l].astype(BF16)
    wo = w_mem_o[l].astype(BF16)
    x2p = _attend(x1p, kb_p.reshape(bp, mt, d), vb_p.reshape(bp, mt, d), gq, wq, wo, bt=1, tt=tt_p)
    bt_s = 4 if bs % 4 == 0 else 1
    x2s = _attend(x1s, cache_mem_k[l].astype(BF16).reshape(bs, mt, d),
                  cache_mem_v[l].astype(BF16).reshape(bs, mt, d), gq, wq, wo, bt=bt_s, tt=ts)

    wt = jnp.concatenate([w_router_c[l].T, jnp.zeros((FINE_ROW0 - N_GROUPS, d), F32), w_router_f[l].T], axis=0)
    wth = wt.astype(BF16)
    wtl = (wt - wth.astype(F32)).astype(BF16)
    bias = jnp.concatenate([b_router_c[l], jnp.zeros((FINE_ROW0 - N_GROUPS,), F32), b_router_f[l]]).reshape(-1, 1)
    moe_p = dict(g=row(norm_ffn[l]), wth=wth, wtl=wtl, bias=bias, w1=w_exp_1[l], w3=w_exp_3[l], w2=w_exp_2[l])
    y_p, y_s = _moe_final(x2p, x2s, moe_p, row(norm_final))

    heads_b = v_norm_g.shape[1]
    hd_m = d // MEM_HEADS
    return (y_p.reshape(bp, tp, d), y_s.reshape(bs, ts, d),
            conv_p[None], h_p.reshape(1, bp, d_a),
            k_p.reshape(1, bp, mt, MEM_HEADS, hd_m), v_p.reshape(1, bp, mt, MEM_HEADS, hd_m),
            conv_s[None], h_s.reshape(1, bs, d_a),
            vn_s.reshape(1, bs, ts, heads_b, B_HD))
```

```python
import functools

import jax
import jax.numpy as jnp
from jax import lax
from jax.experimental import pallas as pl
from jax.experimental.pallas import tpu as pltpu

EPS = 1e-6
LRU_C = 8.0
CONV_W = 4
A_HEADS = 16
B_HD = 128
MLP_CHUNK = 128
MEM_HEADS = 4
N_GROUPS = 4
EXPERTS_PER_GROUP = 8
N_EXPERTS = N_GROUPS * EXPERTS_PER_GROUP
PAST_LEN = 1024

BF16 = jnp.bfloat16
F32 = jnp.float32

V7X_VMEM_BYTES = 64 * 1024 * 1024
V7X_MXU_DIM = 256
SUBLANES = 8
XA_ROW0 = SUBLANES
FINE_ROW0 = SUBLANES

MIX_ROWS = 256
ROUTE_ROWS = 512
EXPERT_ROWS = 512
ZERO_ROWS = 128
VMEM_LIMIT = V7X_VMEM_BYTES * 7 // 8


def _cparams(sem):
    return pltpu.CompilerParams(dimension_semantics=sem, vmem_limit_bytes=VMEM_LIMIT)


def _rms(x, g):
    return x * lax.rsqrt(jnp.mean(x * x, axis=-1, keepdims=True) + EPS) * g


def _dot(a, b):
    return jnp.dot(a, b, preferred_element_type=F32)


def _dot_nt(a, b):
    return lax.dot_general(a, b, (((1,), (1,)), ((), ())), preferred_element_type=F32)


def _full(shape):
    nd = len(shape)
    return pl.BlockSpec(shape, lambda *_: (0,) * nd)


def _gelu_proj(xn, w_ref, lo, width):
    cc = V7X_MXU_DIM
    return jnp.concatenate([jax.nn.gelu(_dot(xn, w_ref[:, lo + c * cc:lo + (c + 1) * cc]))
                            for c in range(width // cc)], axis=-1)


def _mixer_kernel(x_ref, conv0_ref, h0_ref, gin_ref, win_ref, cw_ref, cb_ref, wr_ref, br_ref,
                  wi_ref, bi_ref, lam_ref, vg_ref, ws_ref, bst_ref, goa_ref, gob_ref, wout_ref,
                  *rest, bt, tt, chunk, pos0, emit_vn):
    if emit_vn:
        y_ref, convo_ref, ho_ref, vn_ref = rest[:4]
        scratch = rest[4:]
    else:
        y_ref, convo_ref, ho_ref = rest[:3]
        vn_ref = None
        scratch = rest[3:]
    xa_buf, a_buf, u_buf, h_buf, z_buf, hc_ref, ycat = scratch
    t = pl.program_id(1)
    d = x_ref.shape[-1]
    d_a = a_buf.shape[-1]
    d_b = z_buf.shape[-1]
    m = bt * tt

    hist = XA_ROW0 - (CONV_W - 1)

    @pl.when(t == 0)
    def _():
        xa_buf[:, hist:XA_ROW0, :] = conv0_ref[...]
        hc_ref[...] = h0_ref[...]

    x = x_ref[...].reshape(m, d)
    xn = _rms(x, gin_ref[...]).astype(BF16)
    xa = _dot(xn, win_ref[:, 0:d_a])
    gga = _gelu_proj(xn, win_ref, d_a, d_a)
    gub = _gelu_proj(xn, win_ref, 2 * d_a, d_b)
    gv = _gelu_proj(xn, win_ref, 2 * d_a + d_b, d_b)

    xa_buf[:, XA_ROW0:XA_ROW0 + tt, :] = xa.reshape(bt, tt, d_a)
    xc = cb_ref[...][None]
    for k in range(CONV_W):
        xc = xc + xa_buf[:, hist + k:hist + k + tt, :] * cw_ref[k:k + 1, :][None]
    tail = xa_buf[:, tt + hist:tt + XA_ROW0, :]
    convo_ref[...] = tail
    xa_buf[:, hist:XA_ROW0, :] = tail
    xc = xc.reshape(m, d_a)

    xcb = xc.astype(BF16)
    gw = V7X_MXU_DIM
    r_pre = jnp.concatenate([_dot(xcb[:, g * gw:(g + 1) * gw], wr_ref[g]) for g in range(d_a // gw)], axis=-1)
    i_pre = jnp.concatenate([_dot(xcb[:, g * gw:(g + 1) * gw], wi_ref[g]) for g in range(d_a // gw)], axis=-1)
    r = jax.nn.sigmoid(r_pre + br_ref[...])
    ig = jax.nn.sigmoid(i_pre + bi_ref[...])
    nl = -lam_ref[...]
    softplus = jnp.maximum(nl, 0.0) + jnp.log1p(jnp.exp(-jnp.abs(nl)))
    log_a = -LRU_C * r * softplus
    a = jnp.exp(log_a)
    m2 = -jnp.tanh(log_a) * (1.0 + a * a)
    mult = jnp.where(m2 > 0.0, m2 * lax.rsqrt(m2), 0.0)
    if pos0 == 0:
        row = lax.broadcasted_iota(jnp.int32, (m, d_a), 0)
        is_first = jnp.logical_and(t == 0, (row % tt) == 0)
        mult = jnp.where(is_first, 1.0, mult)
    u = mult * (ig * xc)
    a_buf[...] = a.reshape(bt, tt, d_a)
    u_buf[...] = u.reshape(bt, tt, d_a)

    sub = lax.broadcasted_iota(jnp.int32, (SUBLANES, d_a), 0)
    for b in range(bt):
        h = hc_ref[b]
        for j in range(tt // SUBLANES):
            r0 = j * SUBLANES
            av = a_buf[b, r0:r0 + SUBLANES, :]
            uv = u_buf[b, r0:r0 + SUBLANES, :]
            for s in (1, 2, 4):
                a_sh = jnp.where(sub >= s, pltpu.roll(av, s, 0), 1.0)
                u_sh = jnp.where(sub >= s, pltpu.roll(uv, s, 0), 0.0)
                uv = av * u_sh + uv
                av = av * a_sh
            hv = av * h + uv
            h_buf[b, r0:r0 + SUBLANES, :] = hv
            h = hv[SUBLANES - 1:SUBLANES, :]
        hc_ref[b] = h
    ho_ref[...] = hc_ref[...]

    ya = h_buf[...].reshape(m, d_a) * gga
    ycat[:, 0:d_a] = _rms(ya, goa_ref[...]).astype(BF16)

    heads = d_b // B_HD
    ri = lax.broadcasted_iota(jnp.int32, (chunk, chunk), 0)
    ci = lax.broadcasted_iota(jnp.int32, (chunk, chunk), 1)
    for hh in range(heads):
        lo = hh * B_HD
        gvh = gv[:, lo:lo + B_HD]
        vn_h = gvh * lax.rsqrt(jnp.mean(gvh * gvh, axis=-1, keepdims=True) + EPS) * vg_ref[:, lo:lo + B_HD]
        if emit_vn:
            vn_ref[:, :, lo:lo + B_HD] = vn_h.reshape(bt, tt, B_HD)
        vn_hb = vn_h.astype(BF16)
        ws_h = jnp.where(ci <= ri, ws_ref[hh], 0.0).astype(BF16)
        bias = bst_ref[:, hh:hh + 1]
        for c in range(m // chunk):
            z = _dot(ws_h, vn_hb[c * chunk:(c + 1) * chunk, :]) + bias
            z_buf[c * chunk:(c + 1) * chunk, lo:lo + B_HD] = z
    yb = gub * z_buf[...]
    ycat[:, d_a:] = _rms(yb, gob_ref[...]).astype(BF16)

    y_ref[...] = (x + _dot(ycat[...], wout_ref[...])).reshape(bt, tt, d)


def _mixer(x, conv0, h0, p, *, bt, tt, pos0, emit_vn):
    nb, seq, d = x.shape
    d_a = conv0.shape[-1]
    d_b = p['vg'].shape[-1]
    chunk = min(tt, MLP_CHUNK)
    assert nb % bt == 0 and seq % tt == 0 and tt % chunk == 0 and tt % SUBLANES == 0
    grid = (nb // bt, seq // tt)
    weights = [p['gin'], p['win'], p['cw'], p['cb'], p['wr'], p['br'], p['wi'], p['bi'], p['lam'],
               p['vg'], p['ws'][:, :chunk, :chunk], p['bst'][:chunk], p['goa'], p['gob'], p['wout']]
    in_specs = [pl.BlockSpec((bt, tt, d), lambda b, t: (b, t, 0)),
                pl.BlockSpec((bt, CONV_W - 1, d_a), lambda b, t: (b, 0, 0)),
                pl.BlockSpec((bt, 1, d_a), lambda b, t: (b, 0, 0))]
    in_specs += [_full(w.shape) for w in weights]
    out_shape = [jax.ShapeDtypeStruct((nb, seq, d), F32),
                 jax.ShapeDtypeStruct((nb, CONV_W - 1, d_a), F32),
                 jax.ShapeDtypeStruct((nb, 1, d_a), F32)]
    out_specs = [pl.BlockSpec((bt, tt, d), lambda b, t: (b, t, 0)),
                 pl.BlockSpec((bt, CONV_W - 1, d_a), lambda b, t: (b, 0, 0)),
                 pl.BlockSpec((bt, 1, d_a), lambda b, t: (b, 0, 0))]
    if emit_vn:
        out_shape.append(jax.ShapeDtypeStruct((nb, seq, d_b), F32))
        out_specs.append(pl.BlockSpec((bt, tt, d_b), lambda b, t: (b, t, 0)))
    scratch = [pltpu.VMEM((bt, tt + XA_ROW0, d_a), F32), pltpu.VMEM((bt, tt, d_a), F32),
               pltpu.VMEM((bt, tt, d_a), F32), pltpu.VMEM((bt, tt, d_a), F32),
               pltpu.VMEM((bt * tt, d_b), F32), pltpu.VMEM((bt, 1, d_a), F32),
               pltpu.VMEM((bt * tt, d), BF16)]
    kern = functools.partial(_mixer_kernel, bt=bt, tt=tt, chunk=chunk, pos0=pos0, emit_vn=emit_vn)
    return pl.pallas_call(
        kern, grid=grid, in_specs=in_specs, out_specs=out_specs, out_shape=out_shape,
        scratch_shapes=scratch, compiler_params=_cparams(("arbitrary", "arbitrary")),
        name="mixer")(x, conv0, h0, *weights)


def _memkv_kernel(m_ref, g_ref, wk_ref, wv_ref, k_ref, v_ref, kb_ref, vb_ref):
    mn = _rms(m_ref[...], g_ref[...]).astype(BF16)
    k = _dot(mn, wk_ref[...])
    v = _dot(mn, wv_ref[...])
    hd = k_ref.shape[-1]
    for hh in range(k_ref.shape[1]):
        k_ref[:, hh, :] = k[:, hh * hd:(hh + 1) * hd]
        v_ref[:, hh, :] = v[:, hh * hd:(hh + 1) * hd]
    kb_ref[...] = k.astype(BF16)
    vb_ref[...] = v.astype(BF16)


def _mem_kv(mem2d, g, wk, wv, *, tm=256):
    n, d = mem2d.shape
    hd = d // MEM_HEADS
    f_spec = pl.BlockSpec((tm, MEM_HEADS, hd), lambda i: (i, 0, 0))
    b_spec = pl.BlockSpec((tm, d), lambda i: (i, 0))
    return pl.pallas_call(
        _memkv_kernel, grid=(n // tm,),
        in_specs=[pl.BlockSpec((tm, d), lambda i: (i, 0)), _full(g.shape), _full(wk.shape), _full(wv.shape)],
        out_specs=[f_spec, f_spec, b_spec, b_spec],
        out_shape=[jax.ShapeDtypeStruct((n, MEM_HEADS, hd), F32), jax.ShapeDtypeStruct((n, MEM_HEADS, hd), F32),
                   jax.ShapeDtypeStruct((n, d), BF16), jax.ShapeDtypeStruct((n, d), BF16)],
        compiler_params=_cparams(("arbitrary",)), name="mem_kv")(mem2d, g, wk, wv)


def _attend_kernel(x_ref, k_ref, v_ref, g_ref, wq_ref, wo_ref, y_ref, o_buf, *, bt, tt):
    d = x_ref.shape[-1]
    hd = d // MEM_HEADS
    x = x_ref[...]
    xn = _rms(x, g_ref[...]).astype(BF16)
    q = _dot(xn, wq_ref[...]).astype(BF16)
    scale = hd ** -0.5
    for b in range(bt):
        for hh in range(MEM_HEADS):
            qh = q[b * tt:(b + 1) * tt, hh * hd:(hh + 1) * hd]
            s = _dot_nt(qh, k_ref[b, :, hh * hd:(hh + 1) * hd]) * scale
            e = jnp.exp(s - jnp.max(s, axis=-1, keepdims=True))
            pr = e / jnp.sum(e, axis=-1, keepdims=True)
            o = _dot(pr.astype(BF16), v_ref[b, :, hh * hd:(hh + 1) * hd])
            o_buf[b * tt:(b + 1) * tt, hh * hd:(hh + 1) * hd] = o.astype(BF16)
    y_ref[...] = x + _dot(o_buf[...], wo_ref[...])


def _attend(x, k, v, g, wq, wo, *, bt, tt):
    nb, seq, d = x.shape
    mt = k.shape[1]
    m = bt * tt
    n = nb * seq
    assert n % m == 0 and (bt == 1 or tt == seq)
    kern = functools.partial(_attend_kernel, bt=bt, tt=tt)
    kv_spec = pl.BlockSpec((bt, mt, d), lambda i: ((i * m) // (bt * seq), 0, 0))
    return pl.pallas_call(
        kern, grid=(n // m,),
        in_specs=[pl.BlockSpec((m, d), lambda i: (i, 0)), kv_spec, kv_spec,
                  _full(g.shape), _full(wq.shape), _full(wo.shape)],
        out_specs=pl.BlockSpec((m, d), lambda i: (i, 0)),
        out_shape=jax.ShapeDtypeStruct((n, d), F32),
        scratch_shapes=[pltpu.VMEM((m, d), BF16)],
        compiler_params=_cparams(("arbitrary",)), name="attend")(x.reshape(n, d), k, v, g, wq, wo)


def _two_source_specs(tt, d, npb):
    return [pl.BlockSpec((tt, d), lambda i, *_: (jnp.minimum(i, npb - 1), 0)),
            pl.BlockSpec((tt, d), lambda i, *_: (jnp.maximum(i - npb, 0), 0))]


def _router_kernel(xp_ref, xs_ref, g_ref, wth_ref, wtl_ref, bias_ref, eid_ref, gate_ref, rank_ref, cnt_ref, *, npb):
    i = pl.program_id(0)
    tt = xp_ref.shape[0]

    @pl.when(i == 0)
    def _():
        cnt_ref[...] = jnp.zeros_like(cnt_ref)

    x = jnp.where(i < npb, xp_ref[...], xs_ref[...])
    xn = _rms(x, g_ref[...])
    xh = xn.astype(BF16)
    xl = (xn - xh.astype(F32)).astype(BF16)
    logits = (_dot_nt(wth_ref[...], xh) + _dot_nt(wth_ref[...], xl) + _dot_nt(wtl_ref[...], xh)) + bias_ref[...]
    lc = logits[0:N_GROUPS]
    ec = jnp.exp(lc - jnp.max(lc, axis=0, keepdims=True))
    pc = ec / jnp.sum(ec, axis=0, keepdims=True)
    p_grp = jnp.max(pc, axis=0, keepdims=True)
    gi = lax.broadcasted_iota(jnp.int32, pc.shape, 0)
    grp = jnp.min(jnp.where(pc == p_grp, gi, N_GROUPS), axis=0, keepdims=True)
    lf = jnp.zeros((EXPERTS_PER_GROUP, tt), F32)
    for gidx in range(N_GROUPS):
        lo = FINE_ROW0 + gidx * EXPERTS_PER_GROUP
        lf = lf + jnp.where(grp == gidx, logits[lo:lo + EXPERTS_PER_GROUP], 0.0)
    fi = lax.broadcasted_iota(jnp.int32, lf.shape, 0)
    t1 = jnp.max(lf, axis=0, keepdims=True)
    i1 = jnp.min(jnp.where(lf == t1, fi, EXPERTS_PER_GROUP), axis=0, keepdims=True)
    lf2 = jnp.where(fi == i1, -jnp.inf, lf)
    t2 = jnp.max(lf2, axis=0, keepdims=True)
    i2 = jnp.min(jnp.where(lf2 == t2, fi, EXPERTS_PER_GROUP), axis=0, keepdims=True)
    e2 = jnp.exp(t2 - t1)
    den = 1.0 + e2
    gate_ref[0:1, :] = p_grp * (1.0 / den)
    gate_ref[1:2, :] = p_grp * (e2 / den)
    eid1 = grp * EXPERTS_PER_GROUP + i1
    eid2 = grp * EXPERTS_PER_GROUP + i2
    eid_ref[0:1, :] = eid1
    eid_ref[1:2, :] = eid2

    ei = lax.broadcasted_iota(jnp.int32, (N_EXPERTS, tt), 0)
    oh1 = ei == eid1
    oh2 = ei == eid2
    oh = jnp.where(jnp.logical_or(oh1, oh2), 1.0, 0.0)
    tr = lax.broadcasted_iota(jnp.int32, (tt, tt), 0)
    tc = lax.broadcasted_iota(jnp.int32, (tt, tt), 1)
    upper = jnp.where(tr < tc, 1.0, 0.0).astype(BF16)
    run = cnt_ref[:, 0:1]
    before = _dot(oh.astype(BF16), upper) + run
    rank_ref[0:1, :] = jnp.sum(jnp.where(oh1, before, 0.0), axis=0, keepdims=True).astype(jnp.int32)
    rank_ref[1:2, :] = jnp.sum(jnp.where(oh2, before, 0.0), axis=0, keepdims=True).astype(jnp.int32)
    cnt_ref[...] = cnt_ref[...] + jnp.sum(oh, axis=1, keepdims=True)


def _router(xp, xs, g, wth, wtl, bias, *, tt):
    (n_p, d), n_s = xp.shape, xs.shape[0]
    npb, n = n_p // tt, n_p + n_s
    tok = pl.BlockSpec((2, tt), lambda i: (0, i))
    return pl.pallas_call(
        functools.partial(_router_kernel, npb=npb), grid=(n // tt,),
        in_specs=_two_source_specs(tt, d, npb) + [_full(g.shape), _full(wth.shape), _full(wtl.shape),
                                                  _full(bias.shape)],
        out_specs=[tok, tok, tok, pl.BlockSpec((N_EXPERTS, 128), lambda i: (0, 0))],
        out_shape=[jax.ShapeDtypeStruct((2, n), jnp.int32),
                   jax.ShapeDtypeStruct((2, n), F32),
                   jax.ShapeDtypeStruct((2, n), jnp.int32),
                   jax.ShapeDtypeStruct((N_EXPERTS, 128), F32)],
        compiler_params=_cparams(("arbitrary",)), name="router")(xp, xs, g, wth, wtl, bias)


def _dispatch_kernel(zlo_ref, zhi_ref, dest_ref, xp_hbm, xs_hbm, buf_ref, zero_buf, xbuf, fsem, ssem, zsem,
                     *, npb, nt):
    i = pl.program_id(0)
    tt = xbuf.shape[1]

    def fetch(t, slot):
        @pl.when(t < npb)
        def _():
            pltpu.make_async_copy(xp_hbm.at[pl.ds(pl.multiple_of(t * tt, tt), tt), :],
                                  xbuf.at[slot], fsem.at[slot]).start()

        @pl.when(t >= npb)
        def _():
            pltpu.make_async_copy(xs_hbm.at[pl.ds(pl.multiple_of((t - npb) * tt, tt), tt), :],
                                  xbuf.at[slot], fsem.at[slot]).start()

    def fetch_wait(slot):
        pltpu.make_async_copy(xp_hbm.at[pl.ds(0, tt), :], xbuf.at[slot], fsem.at[slot]).wait()

    def scatter_wait(slot):
        for k in range(2):
            pltpu.make_async_copy(xbuf.at[slot], buf_ref.at[pl.ds(0, tt), :], ssem.at[slot]).wait()

    @pl.when(i == 0)
    def _():
        zero_buf[...] = jnp.zeros_like(zero_buf)
        zc = zero_buf.shape[0]

        def zcopy(chunk):
            return pltpu.make_async_copy(zero_buf, buf_ref.at[pl.ds(pl.multiple_of(chunk * zc, zc), zc), :], zsem)

        def zstart(chunk, c):
            zcopy(chunk).start()
            return c

        def zwait(chunk, c):
            zcopy(chunk).wait()
            return c

        for e in range(zlo_ref.shape[0]):
            lax.fori_loop(zlo_ref[e], zhi_ref[e], zstart, 0)
        for e in range(zlo_ref.shape[0]):
            lax.fori_loop(zlo_ref[e], zhi_ref[e], zwait, 0)
        fetch(0, 0)

    slot = i % 3
    nxt = (i + 1) % 3

    @pl.when(i >= 2)
    def _():
        scatter_wait(nxt)

    @pl.when(i + 1 < nt)
    def _():
        fetch(i + 1, nxt)

    fetch_wait(slot)
    x_ref = xbuf.at[slot]
    s_sem = ssem.at[slot]

    def group(gidx, c):
        r0 = pl.multiple_of(gidx * SUBLANES, SUBLANES)
        rows = x_ref.at[pl.ds(r0, SUBLANES), :]
        for j in range(SUBLANES):
            for k in range(2):
                pltpu.make_async_copy(rows.at[pl.ds(j, 1), :],
                                      buf_ref.at[pl.ds(dest_ref[2 * r0 + (2 * j + k)], 1), :],
                                      s_sem).start(priority=k)
        return c

    lax.fori_loop(0, tt // SUBLANES, group, 0)

    @pl.when(i == nt - 1)
    def _():
        if nt > 1:
            scatter_wait((i + 2) % 3)
        scatter_wait(slot)


def _dispatch(pad_lo, pad_hi, dest, xp, xs, p_rows, *, tt):
    (n_p, d), n_s = xp.shape, xs.shape[0]
    npb, n = n_p // tt, n_p + n_s
    nt = n // tt
    zlo = (pad_lo // ZERO_ROWS).astype(jnp.int32)
    zhi = (pad_hi // ZERO_ROWS).astype(jnp.int32)
    gs = pltpu.PrefetchScalarGridSpec(
        num_scalar_prefetch=2, grid=(nt,),
        in_specs=[pl.BlockSpec((2 * tt,), lambda i, *_: (i,), memory_space=pltpu.SMEM),
                  pl.BlockSpec(memory_space=pl.ANY), pl.BlockSpec(memory_space=pl.ANY)],
        out_specs=pl.BlockSpec(memory_space=pl.ANY),
        scratch_shapes=[pltpu.VMEM((ZERO_ROWS, d), F32), pltpu.VMEM((3, tt, d), F32),
                        pltpu.SemaphoreType.DMA((3,)), pltpu.SemaphoreType.DMA((3,)),
                        pltpu.SemaphoreType.DMA(())])
    return pl.pallas_call(
        functools.partial(_dispatch_kernel, npb=npb, nt=nt), grid_spec=gs,
        out_shape=jax.ShapeDtypeStruct((p_rows, d), F32),
        compiler_params=_cparams(("arbitrary",)), name="dispatch")(zlo, zhi, dest, xp, xs)


def _experts_kernel(be_ref, nu_ref, x_ref, g_ref, w1_ref, w3_ref, w2_ref, o_ref, w1b, w3b, w2b):
    i = pl.program_id(0)

    @pl.when(i < nu_ref[0])
    def _():
        prev = be_ref[jnp.maximum(i - 1, 0)]

        @pl.when(jnp.logical_or(i == 0, be_ref[i] != prev))
        def _():
            w1b[...] = w1_ref[0].astype(BF16)
            w3b[...] = w3_ref[0].astype(BF16)
            w2b[...] = w2_ref[0].astype(BF16)

        x = x_ref[...]
        xg = (x * g_ref[...]).astype(BF16)
        scale = lax.rsqrt(jnp.mean(x * x, axis=-1, keepdims=True) + EPS)
        h1 = _dot(xg, w1b[...]) * scale
        h3 = _dot(xg, w3b[...]) * scale
        hd = (h1 * jax.nn.sigmoid(h1) * h3).astype(BF16)
        o_ref[...] = _dot(hd, w2b[...])

    @pl.when(i >= nu_ref[0])
    def _():
        o_ref[...] = jnp.zeros_like(o_ref)


def _experts(block_e, n_used, buf, g, w1, w3, w2, *, bm):
    p_rows, d = buf.shape
    ne, _, de = w1.shape
    nblk = p_rows // bm
    gs = pltpu.PrefetchScalarGridSpec(
        num_scalar_prefetch=2, grid=(nblk,),
        in_specs=[pl.BlockSpec((bm, d), lambda i, be, nu: (jnp.minimum(i, nu[0] - 1), 0)),
                  pl.BlockSpec(g.shape, lambda i, be, nu: (0, 0)),
                  pl.BlockSpec((1, d, de), lambda i, be, nu: (be[i], 0, 0)),
                  pl.BlockSpec((1, d, de), lambda i, be, nu: (be[i], 0, 0)),
                  pl.BlockSpec((1, de, d), lambda i, be, nu: (be[i], 0, 0))],
        out_specs=pl.BlockSpec((bm, d), lambda i, be, nu: (i, 0)),
        scratch_shapes=[pltpu.VMEM((d, de), BF16), pltpu.VMEM((d, de), BF16), pltpu.VMEM((de, d), BF16)])
    return pl.pallas_call(
        _experts_kernel, grid_spec=gs,
        out_shape=jax.ShapeDtypeStruct((p_rows, d), F32),
        compiler_params=_cparams(("arbitrary",)), name="experts")(block_e, n_used, buf, g, w1, w3, w2)


def _combine_kernel(dest_ref, x_ref, gate_ref, g_ref, yb_ref, o_ref, a0, b0, a1, b1, sems, *, last):
    i = pl.program_id(0)
    tt = x_ref.shape[0]
    slots = ((a0, b0), (a1, b1))

    def issue(slot):
        for r in range(tt):
            for k in range(2):
                pltpu.make_async_copy(yb_ref.at[pl.ds(dest_ref[2 * r + k], 1), :],
                                      slots[slot][k].at[pl.ds(r, 1), :],
                                      sems.at[slot]).start(priority=k)

    def wait(slot):
        for k in range(2):
            pltpu.make_async_copy(yb_ref.at[pl.ds(0, tt), :], slots[slot][k], sems.at[slot]).wait()

    @pl.when(i == 0)
    def _():
        issue(0)

    for par in range(2):
        @pl.when(jnp.logical_and(i >= 1, i % 2 == par))
        def _(par=par):
            cur = 1 - par
            wait(cur)
            issue(par)
            moe = gate_ref[:, 0:1] * slots[cur][0][...] + gate_ref[:, 1:2] * slots[cur][1][...]
            o_ref[...] = _rms(x_ref[...] + moe, g_ref[...])

            if par == last % 2:
                @pl.when(i == last)
                def _():
                    wait(par)


def _combine(dest_flat, x2d, gate_t, g, yb, *, tt, row_off):
    n, d = x2d.shape
    off = row_off // tt
    nt = n // tt
    prev = lambda i: jnp.maximum(i - 1, 0)
    return pl.pallas_call(
        functools.partial(_combine_kernel, last=nt), grid=(nt + 1,),
        in_specs=[pl.BlockSpec((2 * tt,), lambda i: (jnp.minimum(i, nt - 1) + off,), memory_space=pltpu.SMEM),
                  pl.BlockSpec((tt, d), lambda i: (prev(i), 0)),
                  pl.BlockSpec((tt, 2), lambda i: (prev(i) + off, 0)),
                  _full(g.shape),
                  pl.BlockSpec(memory_space=pl.ANY)],
        out_specs=pl.BlockSpec((tt, d), lambda i: (prev(i), 0)),
        out_shape=jax.ShapeDtypeStruct((n, d), F32),
        scratch_shapes=[pltpu.VMEM((tt, d), F32)] * 4 + [pltpu.SemaphoreType.DMA((2,))],
        compiler_params=_cparams(("arbitrary",)), name="combine")(dest_flat, x2d, gate_t, g, yb)


def _moe_final(xp, xs, mp, g_final):
    (n_p, d), n_s = xp.shape, xs.shape[0]
    n = n_p + n_s
    tt = min(ROUTE_ROWS, n_s)
    assert n_p % tt == 0 and n_s % tt == 0
    eid, gate, rank, cnt = _router(xp, xs, mp['g'], mp['wth'], mp['wtl'], mp['bias'], tt=tt)
    bm = EXPERT_ROWS
    counts = cnt[:, 0].astype(jnp.int32)
    padded = (counts + bm - 1) // bm * bm
    pend = jnp.cumsum(padded)
    pstart = pend - padded
    sel = eid[:, :, None] == jnp.arange(N_EXPERTS, dtype=jnp.int32)
    dest = rank + jnp.sum(jnp.where(sel, pstart, 0), axis=-1)
    dest = dest.T.reshape(-1)
    p_rows = (2 * n + N_EXPERTS * (bm - 1)) // bm * bm
    nblk = p_rows // bm
    block_start = jnp.arange(nblk, dtype=jnp.int32) * bm
    n_used = (pend[-1] // bm).astype(jnp.int32)
    block_e = jnp.sum((pend[None, :] <= block_start[:, None]).astype(jnp.int32), axis=1)
    last_e = jnp.sum((pend <= pend[-1] - 1).astype(jnp.int32))
    block_e = jnp.minimum(block_e, last_e).astype(jnp.int32)
    n_used = n_used.reshape(1)
    pad_lo = jnp.concatenate([pstart + counts, pend[-1:]])
    pad_hi = jnp.concatenate([pend, jnp.full((1,), p_rows, jnp.int32)])
    buf = _dispatch(pad_lo, pad_hi, dest, xp, xs, p_rows, tt=tt)
    yb = _experts(block_e, n_used, buf, mp['g'], mp['w1'], mp['w3'], mp['w2'], bm=bm)
    gate_t = gate.T
    y_p = _combine(dest, xp, gate_t, g_final, yb, tt=tt, row_off=0)
    y_s = _combine(dest, xs, gate_t, g_final, yb, tt=tt, row_off=n_p)
    return y_p, y_s


def _block_diag_pack(w, per):
    h, hd, _ = w.shape
    wg = w.reshape(h // per, per, hd, hd)
    eye = jnp.eye(per, dtype=w.dtype)
    return jnp.einsum('gpij,pq->gpiqj', wg, eye).reshape(h // per, per * hd, per * hd)


def kernel(x_prompt, x_sample, mem_prompt, state_conv_a, state_rglru_h, cache_mem_k, cache_mem_v, norm_mix, w_in, conv_a_w, conv_a_b, lru_w_r, lru_b_r, lru_w_i, lru_b_i, lru_lambda, v_norm_g, mlp_w_s, mlp_b_s, out_norm_a, out_norm_b, w_out, norm_mem_q, norm_mem_kv, w_mem_q, w_mem_k, w_mem_v, w_mem_o, norm_ffn, w_router_c, b_router_c, w_router_f, b_router_f, w_exp_1, w_exp_3, w_exp_2, norm_final):
    depth = w_in.shape[0]
    assert depth == 1
    l = 0
    bp, tp, d = x_prompt.shape
    bs, ts, _ = x_sample.shape
    d_a = conv_a_b.shape[-1]
    a_hd = lru_w_r.shape[-1]
    per = V7X_MXU_DIM // a_hd
    row = lambda v: v.reshape(1, -1)

    mix_p = dict(
        gin=row(norm_mix[l]), win=w_in[l].astype(BF16), cw=conv_a_w[l], cb=row(conv_a_b[l]),
        wr=_block_diag_pack(lru_w_r[l], per).astype(BF16), br=row(lru_b_r[l]),
        wi=_block_diag_pack(lru_w_i[l], per).astype(BF16), bi=row(lru_b_i[l]),
        lam=row(lru_lambda[l]), vg=row(v_norm_g[l]), ws=mlp_w_s[l], bst=mlp_b_s[l].T,
        goa=row(out_norm_a[l]), gob=row(out_norm_b[l]), wout=w_out[l].astype(BF16))

    conv0 = jnp.zeros((bp, CONV_W - 1, d_a), F32)
    h0 = jnp.zeros((bp, 1, d_a), F32)
    tt_p = min(MIX_ROWS, tp)
    x1p, conv_p, h_p = _mixer(x_prompt, conv0, h0, mix_p, bt=1, tt=tt_p, pos0=0, emit_vn=False)
    bt_mix = 8 if bs % 8 == 0 else 1
    x1s, conv_s, h_s, vn_s = _mixer(x_sample, state_conv_a[l], state_rglru_h[l][:, None, :], mix_p,
                                    bt=bt_mix, tt=ts, pos0=PAST_LEN, emit_vn=True)

    mt = mem_prompt.shape[1]
    k_p, v_p, kb_p, vb_p = _mem_kv(mem_prompt.reshape(bp * mt, d), row(norm_mem_kv[l]),
                                   w_mem_k[l].astype(BF16), w_mem_v[l].astype(BF16))
    gq = row(norm_mem_q[l])
    wq = w_mem_q[l].astype(BF16)
    wo = w_mem_o[l].astype(BF16)
    x2p = _attend(x1p, kb_p.reshape(bp, mt, d), vb_p.reshape(bp, mt, d), gq, wq, wo, bt=1, tt=tt_p)
    bt_s = 4 if bs % 4 == 0 else 1
    x2s = _attend(x1s, cache_mem_k[l].astype(BF16).reshape(bs, mt, d),
                  cache_mem_v[l].astype(BF16).reshape(bs, mt, d), gq, wq, wo, bt=bt_s, tt=ts)

    wt = jnp.concatenate([w_router_c[l].T, jnp.zeros((FINE_ROW0 - N_GROUPS, d), F32), w_router_f[l].T], axis=0)
    wth = wt.astype(BF16)
    wtl = (wt - wth.astype(F32)).astype(BF16)
    bias = jnp.concatenate([b_router_c[l], jnp.zeros((FINE_ROW0 - N_GROUPS,), F32), b_router_f[l]]).reshape(-1, 1)
    moe_p = dict(g=row(norm_ffn[l]), wth=wth, wtl=wtl, bias=bias, w1=w_exp_1[l], w3=w_exp_3[l], w2=w_exp_2[l])
    y_p, y_s = _moe_final(x2p, x2s, moe_p, row(norm_final))

    heads_b = v_norm_g.shape[1]
    hd_m = d // MEM_HEADS
    return (y_p.reshape(bp, tp, d), y_s.reshape(bs, ts, d),
            conv_p[None], h_p.reshape(1, bp, d_a),
            k_p.reshape(1, bp, mt, MEM_HEADS, hd_m), v_p.reshape(1, bp, mt, MEM_HEADS, hd_m),
            conv_s[None], h_s.reshape(1, bs, d_a),
            vn_s.reshape(1, bs, ts, heads_b, B_HD))
```

```python
import functools

import jax
import jax.numpy as jnp
from jax import lax
from jax.experimental import pallas as pl
from jax.experimental.pallas import tpu as pltpu

EPS = 1e-6
LRU_C = 8.0
CONV_W = 4
A_HEADS = 16
B_HD = 128
MLP_CHUNK = 128
MEM_HEADS = 4
N_GROUPS = 4
EXPERTS_PER_GROUP = 8
N_EXPERTS = N_GROUPS * EXPERTS_PER_GROUP
PAST_LEN = 1024

BF16 = jnp.bfloat16
F32 = jnp.float32

V7X_VMEM_BYTES = 64 * 1024 * 1024
V7X_MXU_DIM = 256
SUBLANES = 8
XA_ROW0 = SUBLANES
FINE_ROW0 = SUBLANES

MIX_ROWS = 256
ATTEND_ROWS = 512
ROUTE_ROWS = 512
EXPERT_ROWS = 512
ZERO_ROWS = 128
VMEM_LIMIT = V7X_VMEM_BYTES * 7 // 8


def _cparams(sem):
    return pltpu.CompilerParams(dimension_semantics=sem, vmem_limit_bytes=VMEM_LIMIT)


def _rms(x, g):
    return x * lax.rsqrt(jnp.mean(x * x, axis=-1, keepdims=True) + EPS) * g


def _dot(a, b):
    return jnp.dot(a, b, preferred_element_type=F32)


def _dot_nt(a, b):
    return lax.dot_general(a, b, (((1,), (1,)), ((), ())), preferred_element_type=F32)


def _full(shape):
    nd = len(shape)
    return pl.BlockSpec(shape, lambda *_: (0,) * nd)


def _gelu_proj(xn, w_ref, lo, width):
    cc = V7X_MXU_DIM
    return jnp.concatenate([jax.nn.gelu(_dot(xn, w_ref[:, lo + c * cc:lo + (c + 1) * cc]))
                            for c in range(width // cc)], axis=-1)


def _mixer_kernel(x_ref, conv0_ref, h0_ref, gin_ref, win_ref, cw_ref, cb_ref, wr_ref, br_ref,
                  wi_ref, bi_ref, lam_ref, vg_ref, ws_ref, bst_ref, goa_ref, gob_ref, wout_ref,
                  *rest, bt, tt, chunk, pos0, emit_vn):
    if emit_vn:
        y_ref, convo_ref, ho_ref, vn_ref = rest[:4]
        scratch = rest[4:]
    else:
        y_ref, convo_ref, ho_ref = rest[:3]
        vn_ref = None
        scratch = rest[3:]
    xa_buf, a_buf, u_buf, h_buf, z_buf, hc_ref, ycat = scratch
    t = pl.program_id(1)
    d = x_ref.shape[-1]
    d_a = a_buf.shape[-1]
    d_b = z_buf.shape[-1]
    m = bt * tt

    hist = XA_ROW0 - (CONV_W - 1)

    @pl.when(t == 0)
    def _():
        xa_buf[:, hist:XA_ROW0, :] = conv0_ref[...]
        hc_ref[...] = h0_ref[...]

    x = x_ref[...].reshape(m, d)
    xn = _rms(x, gin_ref[...]).astype(BF16)
    xa = _dot(xn, win_ref[:, 0:d_a])
    gga = _gelu_proj(xn, win_ref, d_a, d_a)
    gub = _gelu_proj(xn, win_ref, 2 * d_a, d_b)
    gv = _gelu_proj(xn, win_ref, 2 * d_a + d_b, d_b)

    xa_buf[:, XA_ROW0:XA_ROW0 + tt, :] = xa.reshape(bt, tt, d_a)
    xc = cb_ref[...][None]
    for k in range(CONV_W):
        xc = xc + xa_buf[:, hist + k:hist + k + tt, :] * cw_ref[k:k + 1, :][None]
    tail = xa_buf[:, tt + hist:tt + XA_ROW0, :]
    convo_ref[...] = tail
    xa_buf[:, hist:XA_ROW0, :] = tail
    xc = xc.reshape(m, d_a)

    xcb = xc.astype(BF16)
    gw = V7X_MXU_DIM
    r_pre = jnp.concatenate([_dot(xcb[:, g * gw:(g + 1) * gw], wr_ref[g]) for g in range(d_a // gw)], axis=-1)
    i_pre = jnp.concatenate([_dot(xcb[:, g * gw:(g + 1) * gw], wi_ref[g]) for g in range(d_a // gw)], axis=-1)
    r = jax.nn.sigmoid(r_pre + br_ref[...])
    ig = jax.nn.sigmoid(i_pre + bi_ref[...])
    nl = -lam_ref[...]
    softplus = jnp.maximum(nl, 0.0) + jnp.log1p(jnp.exp(-jnp.abs(nl)))
    log_a = -LRU_C * r * softplus
    a = jnp.exp(log_a)
    m2 = -jnp.tanh(log_a) * (1.0 + a * a)
    mult = jnp.where(m2 > 0.0, m2 * lax.rsqrt(m2), 0.0)
    if pos0 == 0:
        row = lax.broadcasted_iota(jnp.int32, (m, d_a), 0)
        is_first = jnp.logical_and(t == 0, (row % tt) == 0)
        mult = jnp.where(is_first, 1.0, mult)
    u = mult * (ig * xc)
    a_buf[...] = a.reshape(bt, tt, d_a)
    u_buf[...] = u.reshape(bt, tt, d_a)

    sub = lax.broadcasted_iota(jnp.int32, (SUBLANES, d_a), 0)
    for b in range(bt):
        h = hc_ref[b]
        for j in range(tt // SUBLANES):
            r0 = j * SUBLANES
            av = a_buf[b, r0:r0 + SUBLANES, :]
            uv = u_buf[b, r0:r0 + SUBLANES, :]
            for s in (1, 2, 4):
                a_sh = jnp.where(sub >= s, pltpu.roll(av, s, 0), 1.0)
                u_sh = jnp.where(sub >= s, pltpu.roll(uv, s, 0), 0.0)
                uv = av * u_sh + uv
                av = av * a_sh
            hv = av * h + uv
            h_buf[b, r0:r0 + SUBLANES, :] = hv
            h = hv[SUBLANES - 1:SUBLANES, :]
        hc_ref[b] = h
    ho_ref[...] = hc_ref[...]

    ya = h_buf[...].reshape(m, d_a) * gga
    ycat[:, 0:d_a] = _rms(ya, goa_ref[...]).astype(BF16)

    heads = d_b // B_HD
    ri = lax.broadcasted_iota(jnp.int32, (chunk, chunk), 0)
    ci = lax.broadcasted_iota(jnp.int32, (chunk, chunk), 1)
    for hh in range(heads):
        lo = hh * B_HD
        gvh = gv[:, lo:lo + B_HD]
        vn_h = gvh * lax.rsqrt(jnp.mean(gvh * gvh, axis=-1, keepdims=True) + EPS) * vg_ref[:, lo:lo + B_HD]
        if emit_vn:
            vn_ref[:, :, lo:lo + B_HD] = vn_h.reshape(bt, tt, B_HD)
        vn_hb = vn_h.astype(BF16)
        ws_h = jnp.where(ci <= ri, ws_ref[hh], 0.0).astype(BF16)
        bias = bst_ref[:, hh:hh + 1]
        for c in range(m // chunk):
            z = _dot(ws_h, vn_hb[c * chunk:(c + 1) * chunk, :]) + bias
            z_buf[c * chunk:(c + 1) * chunk, lo:lo + B_HD] = z
    yb = gub * z_buf[...]
    ycat[:, d_a:] = _rms(yb, gob_ref[...]).astype(BF16)

    y_ref[...] = (x + _dot(ycat[...], wout_ref[...])).reshape(bt, tt, d)


def _mixer(x, conv0, h0, p, *, bt, tt, pos0, emit_vn):
    nb, seq, d = x.shape
    d_a = conv0.shape[-1]
    d_b = p['vg'].shape[-1]
    chunk = min(tt, MLP_CHUNK)
    assert nb % bt == 0 and seq % tt == 0 and tt % chunk == 0 and tt % SUBLANES == 0
    grid = (nb // bt, seq // tt)
    weights = [p['gin'], p['win'], p['cw'], p['cb'], p['wr'], p['br'], p['wi'], p['bi'], p['lam'],
               p['vg'], p['ws'][:, :chunk, :chunk], p['bst'][:chunk], p['goa'], p['gob'], p['wout']]
    in_specs = [pl.BlockSpec((bt, tt, d), lambda b, t: (b, t, 0)),
                pl.BlockSpec((bt, CONV_W - 1, d_a), lambda b, t: (b, 0, 0)),
                pl.BlockSpec((bt, 1, d_a), lambda b, t: (b, 0, 0))]
    in_specs += [_full(w.shape) for w in weights]
    out_shape = [jax.ShapeDtypeStruct((nb, seq, d), F32),
                 jax.ShapeDtypeStruct((nb, CONV_W - 1, d_a), F32),
                 jax.ShapeDtypeStruct((nb, 1, d_a), F32)]
    out_specs = [pl.BlockSpec((bt, tt, d), lambda b, t: (b, t, 0)),
                 pl.BlockSpec((bt, CONV_W - 1, d_a), lambda b, t: (b, 0, 0)),
                 pl.BlockSpec((bt, 1, d_a), lambda b, t: (b, 0, 0))]
    if emit_vn:
        out_shape.append(jax.ShapeDtypeStruct((nb, seq, d_b), F32))
        out_specs.append(pl.BlockSpec((bt, tt, d_b), lambda b, t: (b, t, 0)))
    scratch = [pltpu.VMEM((bt, tt + XA_ROW0, d_a), F32), pltpu.VMEM((bt, tt, d_a), F32),
               pltpu.VMEM((bt, tt, d_a), F32), pltpu.VMEM((bt, tt, d_a), F32),
               pltpu.VMEM((bt * tt, d_b), F32), pltpu.VMEM((bt, 1, d_a), F32),
               pltpu.VMEM((bt * tt, d), BF16)]
    kern = functools.partial(_mixer_kernel, bt=bt, tt=tt, chunk=chunk, pos0=pos0, emit_vn=emit_vn)
    return pl.pallas_call(
        kern, grid=grid, in_specs=in_specs, out_specs=out_specs, out_shape=out_shape,
        scratch_shapes=scratch, compiler_params=_cparams(("arbitrary", "arbitrary")),
        name="mixer")(x, conv0, h0, *weights)


def _memkv_kernel(m_ref, g_ref, wk_ref, wv_ref, k_ref, v_ref, kb_ref, vb_ref):
    mn = _rms(m_ref[...], g_ref[...]).astype(BF16)
    k = _dot(mn, wk_ref[...])
    v = _dot(mn, wv_ref[...])
    hd = k_ref.shape[-1]
    for hh in range(k_ref.shape[1]):
        k_ref[:, hh, :] = k[:, hh * hd:(hh + 1) * hd]
        v_ref[:, hh, :] = v[:, hh * hd:(hh + 1) * hd]
    kb_ref[...] = k.astype(BF16)
    vb_ref[...] = v.astype(BF16)


def _mem_kv(mem2d, g, wk, wv, *, tm=256):
    n, d = mem2d.shape
    hd = d // MEM_HEADS
    f_spec = pl.BlockSpec((tm, MEM_HEADS, hd), lambda i: (i, 0, 0))
    b_spec = pl.BlockSpec((tm, d), lambda i: (i, 0))
    return pl.pallas_call(
        _memkv_kernel, grid=(n // tm,),
        in_specs=[pl.BlockSpec((tm, d), lambda i: (i, 0)), _full(g.shape), _full(wk.shape), _full(wv.shape)],
        out_specs=[f_spec, f_spec, b_spec, b_spec],
        out_shape=[jax.ShapeDtypeStruct((n, MEM_HEADS, hd), F32), jax.ShapeDtypeStruct((n, MEM_HEADS, hd), F32),
                   jax.ShapeDtypeStruct((n, d), BF16), jax.ShapeDtypeStruct((n, d), BF16)],
        compiler_params=_cparams(("arbitrary",)), name="mem_kv")(mem2d, g, wk, wv)


def _attend_kernel(x_ref, k_ref, v_ref, g_ref, wq_ref, wo_ref, y_ref, o_buf, *, bt, tt):
    d = x_ref.shape[-1]
    hd = d // MEM_HEADS
    x = x_ref[...]
    xn = _rms(x, g_ref[...]).astype(BF16)
    q = _dot(xn, wq_ref[...]).astype(BF16)
    scale = hd ** -0.5
    for b in range(bt):
        for hh in range(MEM_HEADS):
            qh = q[b * tt:(b + 1) * tt, hh * hd:(hh + 1) * hd]
            s = _dot_nt(qh, k_ref[b, :, hh * hd:(hh + 1) * hd]) * scale
            e = jnp.exp(s - jnp.max(s, axis=-1, keepdims=True))
            pr = e / jnp.sum(e, axis=-1, keepdims=True)
            o = _dot(pr.astype(BF16), v_ref[b, :, hh * hd:(hh + 1) * hd])
            o_buf[b * tt:(b + 1) * tt, hh * hd:(hh + 1) * hd] = o.astype(BF16)
    y_ref[...] = x + _dot(o_buf[...], wo_ref[...])


def _attend(x, k, v, g, wq, wo, *, bt, tt):
    nb, seq, d = x.shape
    mt = k.shape[1]
    m = bt * tt
    n = nb * seq
    assert n % m == 0 and (bt == 1 or tt == seq)
    kern = functools.partial(_attend_kernel, bt=bt, tt=tt)
    kv_spec = pl.BlockSpec((bt, mt, d), lambda i: ((i * m) // (bt * seq), 0, 0))
    return pl.pallas_call(
        kern, grid=(n // m,),
        in_specs=[pl.BlockSpec((m, d), lambda i: (i, 0)), kv_spec, kv_spec,
                  _full(g.shape), _full(wq.shape), _full(wo.shape)],
        out_specs=pl.BlockSpec((m, d), lambda i: (i, 0)),
        out_shape=jax.ShapeDtypeStruct((n, d), F32),
        scratch_shapes=[pltpu.VMEM((m, d), BF16)],
        compiler_params=_cparams(("arbitrary",)), name="attend")(x.reshape(n, d), k, v, g, wq, wo)


def _two_source_specs(tt, d, npb):
    return [pl.BlockSpec((tt, d), lambda i, *_: (jnp.minimum(i, npb - 1), 0)),
            pl.BlockSpec((tt, d), lambda i, *_: (jnp.maximum(i - npb, 0), 0))]


def _router_kernel(xp_ref, xs_ref, g_ref, wth_ref, wtl_ref, bias_ref, eid_ref, gate_ref, rank_ref, cnt_ref, *, npb):
    i = pl.program_id(0)
    tt = xp_ref.shape[0]

    @pl.when(i == 0)
    def _():
        cnt_ref[...] = jnp.zeros_like(cnt_ref)

    x = jnp.where(i < npb, xp_ref[...], xs_ref[...])
    xn = _rms(x, g_ref[...])
    xh = xn.astype(BF16)
    xl = (xn - xh.astype(F32)).astype(BF16)
    logits = (_dot_nt(wth_ref[...], xh) + _dot_nt(wth_ref[...], xl) + _dot_nt(wtl_ref[...], xh)) + bias_ref[...]
    lc = logits[0:N_GROUPS]
    ec = jnp.exp(lc - jnp.max(lc, axis=0, keepdims=True))
    pc = ec / jnp.sum(ec, axis=0, keepdims=True)
    p_grp = jnp.max(pc, axis=0, keepdims=True)
    gi = lax.broadcasted_iota(jnp.int32, pc.shape, 0)
    grp = jnp.min(jnp.where(pc == p_grp, gi, N_GROUPS), axis=0, keepdims=True)
    lf = jnp.zeros((EXPERTS_PER_GROUP, tt), F32)
    for gidx in range(N_GROUPS):
        lo = FINE_ROW0 + gidx * EXPERTS_PER_GROUP
        lf = lf + jnp.where(grp == gidx, logits[lo:lo + EXPERTS_PER_GROUP], 0.0)
    fi = lax.broadcasted_iota(jnp.int32, lf.shape, 0)
    t1 = jnp.max(lf, axis=0, keepdims=True)
    i1 = jnp.min(jnp.where(lf == t1, fi, EXPERTS_PER_GROUP), axis=0, keepdims=True)
    lf2 = jnp.where(fi == i1, -jnp.inf, lf)
    t2 = jnp.max(lf2, axis=0, keepdims=True)
    i2 = jnp.min(jnp.where(lf2 == t2, fi, EXPERTS_PER_GROUP), axis=0, keepdims=True)
    e2 = jnp.exp(t2 - t1)
    den = 1.0 + e2
    gate_ref[0:1, :] = p_grp * (1.0 / den)
    gate_ref[1:2, :] = p_grp * (e2 / den)
    eid1 = grp * EXPERTS_PER_GROUP + i1
    eid2 = grp * EXPERTS_PER_GROUP + i2
    eid_ref[0:1, :] = eid1
    eid_ref[1:2, :] = eid2

    ei = lax.broadcasted_iota(jnp.int32, (N_EXPERTS, tt), 0)
    oh1 = ei == eid1
    oh2 = ei == eid2
    oh = jnp.where(jnp.logical_or(oh1, oh2), 1.0, 0.0)
    tr = lax.broadcasted_iota(jnp.int32, (tt, tt), 0)
    tc = lax.broadcasted_iota(jnp.int32, (tt, tt), 1)
    upper = jnp.where(tr < tc, 1.0, 0.0).astype(BF16)
    run = cnt_ref[:, 0:1]
    before = _dot(oh.astype(BF16), upper) + run
    rank_ref[0:1, :] = jnp.sum(jnp.where(oh1, before, 0.0), axis=0, keepdims=True).astype(jnp.int32)
    rank_ref[1:2, :] = jnp.sum(jnp.where(oh2, before, 0.0), axis=0, keepdims=True).astype(jnp.int32)
    cnt_ref[...] = cnt_ref[...] + jnp.sum(oh, axis=1, keepdims=True)


def _router(xp, xs, g, wth, wtl, bias, *, tt):
    (n_p, d), n_s = xp.shape, xs.shape[0]
    npb, n = n_p // tt, n_p + n_s
    tok = pl.BlockSpec((2, tt), lambda i: (0, i))
    return pl.pallas_call(
        functools.partial(_router_kernel, npb=npb), grid=(n // tt,),
        in_specs=_two_source_specs(tt, d, npb) + [_full(g.shape), _full(wth.shape), _full(wtl.shape),
                                                  _full(bias.shape)],
        out_specs=[tok, tok, tok, pl.BlockSpec((N_EXPERTS, 128), lambda i: (0, 0))],
        out_shape=[jax.ShapeDtypeStruct((2, n), jnp.int32),
                   jax.ShapeDtypeStruct((2, n), F32),
                   jax.ShapeDtypeStruct((2, n), jnp.int32),
                   jax.ShapeDtypeStruct((N_EXPERTS, 128), F32)],
        compiler_params=_cparams(("arbitrary",)), name="router")(xp, xs, g, wth, wtl, bias)


def _dispatch_kernel(zlo_ref, zhi_ref, dest_ref, xp_hbm, xs_hbm, buf_ref, zero_buf, xbuf, fsem, ssem, zsem,
                     *, npb, nt):
    i = pl.program_id(0)
    tt = xbuf.shape[1]

    def fetch(t, slot):
        @pl.when(t < npb)
        def _():
            pltpu.make_async_copy(xp_hbm.at[pl.ds(pl.multiple_of(t * tt, tt), tt), :],
                                  xbuf.at[slot], fsem.at[slot]).start()

        @pl.when(t >= npb)
        def _():
            pltpu.make_async_copy(xs_hbm.at[pl.ds(pl.multiple_of((t - npb) * tt, tt), tt), :],
                                  xbuf.at[slot], fsem.at[slot]).start()

    def fetch_wait(slot):
        pltpu.make_async_copy(xp_hbm.at[pl.ds(0, tt), :], xbuf.at[slot], fsem.at[slot]).wait()

    def scatter_wait(slot):
        for k in range(2):
            pltpu.make_async_copy(xbuf.at[slot], buf_ref.at[pl.ds(0, tt), :], ssem.at[slot]).wait()

    @pl.when(i == 0)
    def _():
        zero_buf[...] = jnp.zeros_like(zero_buf)
        zc = zero_buf.shape[0]

        def zcopy(chunk):
            return pltpu.make_async_copy(zero_buf, buf_ref.at[pl.ds(pl.multiple_of(chunk * zc, zc), zc), :], zsem)

        def zstart(chunk, c):
            zcopy(chunk).start()
            return c

        def zwait(chunk, c):
            zcopy(chunk).wait()
            return c

        for e in range(zlo_ref.shape[0]):
            lax.fori_loop(zlo_ref[e], zhi_ref[e], zstart, 0)
        for e in range(zlo_ref.shape[0]):
            lax.fori_loop(zlo_ref[e], zhi_ref[e], zwait, 0)
        fetch(0, 0)

    slot = i % 3
    nxt = (i + 1) % 3

    @pl.when(i >= 2)
    def _():
        scatter_wait(nxt)

    @pl.when(i + 1 < nt)
    def _():
        fetch(i + 1, nxt)

    fetch_wait(slot)
    x_ref = xbuf.at[slot]
    s_sem = ssem.at[slot]

    def group(gidx, c):
        r0 = pl.multiple_of(gidx * SUBLANES, SUBLANES)
        rows = x_ref.at[pl.ds(r0, SUBLANES), :]
        for j in range(SUBLANES):
            for k in range(2):
                pltpu.make_async_copy(rows.at[pl.ds(j, 1), :],
                                      buf_ref.at[pl.ds(dest_ref[2 * r0 + (2 * j + k)], 1), :],
                                      s_sem).start(priority=k)
        return c

    lax.fori_loop(0, tt // SUBLANES, group, 0)

    @pl.when(i == nt - 1)
    def _():
        if nt > 1:
            scatter_wait((i + 2) % 3)
        scatter_wait(slot)


def _dispatch(pad_lo, pad_hi, dest, xp, xs, p_rows, *, tt):
    (n_p, d), n_s = xp.shape, xs.shape[0]
    npb, n = n_p // tt, n_p + n_s
    nt = n // tt
    zlo = (pad_lo // ZERO_ROWS).astype(jnp.int32)
    zhi = (pad_hi // ZERO_ROWS).astype(jnp.int32)
    gs = pltpu.PrefetchScalarGridSpec(
        num_scalar_prefetch=2, grid=(nt,),
        in_specs=[pl.BlockSpec((2 * tt,), lambda i, *_: (i,), memory_space=pltpu.SMEM),
                  pl.BlockSpec(memory_space=pl.ANY), pl.BlockSpec(memory_space=pl.ANY)],
        out_specs=pl.BlockSpec(memory_space=pl.ANY),
        scratch_shapes=[pltpu.VMEM((ZERO_ROWS, d), F32), pltpu.VMEM((3, tt, d), F32),
                        pltpu.SemaphoreType.DMA((3,)), pltpu.SemaphoreType.DMA((3,)),
                        pltpu.SemaphoreType.DMA(())])
    return pl.pallas_call(
        functools.partial(_dispatch_kernel, npb=npb, nt=nt), grid_spec=gs,
        out_shape=jax.ShapeDtypeStruct((p_rows, d), F32),
        compiler_params=_cparams(("arbitrary",)), name="dispatch")(zlo, zhi, dest, xp, xs)


def _experts_kernel(be_ref, nu_ref, x_ref, g_ref, w1_ref, w3_ref, w2_ref, o_ref, w1b, w3b, w2b):
    i = pl.program_id(0)

    @pl.when(i < nu_ref[0])
    def _():
        prev = be_ref[jnp.maximum(i - 1, 0)]

        @pl.when(jnp.logical_or(i == 0, be_ref[i] != prev))
        def _():
            w1b[...] = w1_ref[0].astype(BF16)
            w3b[...] = w3_ref[0].astype(BF16)
            w2b[...] = w2_ref[0].astype(BF16)

        x = x_ref[...]
        xg = (x * g_ref[...]).astype(BF16)
        scale = lax.rsqrt(jnp.mean(x * x, axis=-1, keepdims=True) + EPS)
        h1 = _dot(xg, w1b[...]) * scale
        h3 = _dot(xg, w3b[...]) * scale
        hd = (h1 * jax.nn.sigmoid(h1) * h3).astype(BF16)
        o_ref[...] = _dot(hd, w2b[...])

    @pl.when(i >= nu_ref[0])
    def _():
        o_ref[...] = jnp.zeros_like(o_ref)


def _experts(block_e, n_used, buf, g, w1, w3, w2, *, bm):
    p_rows, d = buf.shape
    ne, _, de = w1.shape
    nblk = p_rows // bm
    gs = pltpu.PrefetchScalarGridSpec(
        num_scalar_prefetch=2, grid=(nblk,),
        in_specs=[pl.BlockSpec((bm, d), lambda i, be, nu: (jnp.minimum(i, nu[0] - 1), 0)),
                  pl.BlockSpec(g.shape, lambda i, be, nu: (0, 0)),
                  pl.BlockSpec((1, d, de), lambda i, be, nu: (be[i], 0, 0)),
                  pl.BlockSpec((1, d, de), lambda i, be, nu: (be[i], 0, 0)),
                  pl.BlockSpec((1, de, d), lambda i, be, nu: (be[i], 0, 0))],
        out_specs=pl.BlockSpec((bm, d), lambda i, be, nu: (i, 0)),
        scratch_shapes=[pltpu.VMEM((d, de), BF16), pltpu.VMEM((d, de), BF16), pltpu.VMEM((de, d), BF16)])
    return pl.pallas_call(
        _experts_kernel, grid_spec=gs,
        out_shape=jax.ShapeDtypeStruct((p_rows, d), F32),
        compiler_params=_cparams(("arbitrary",)), name="experts")(block_e, n_used, buf, g, w1, w3, w2)


def _combine_kernel(dest_ref, x_ref, gate_ref, g_ref, yb_ref, o_ref, a0, b0, a1, b1, sems, *, last):
    i = pl.program_id(0)
    tt = x_ref.shape[0]
    slots = ((a0, b0), (a1, b1))

    def issue(slot):
        for r in range(tt):
            for k in range(2):
                pltpu.make_async_copy(yb_ref.at[pl.ds(dest_ref[2 * r + k], 1), :],
                                      slots[slot][k].at[pl.ds(r, 1), :],
                                      sems.at[slot]).start(priority=k)

    def wait(slot):
        for k in range(2):
            pltpu.make_async_copy(yb_ref.at[pl.ds(0, tt), :], slots[slot][k], sems.at[slot]).wait()

    @pl.when(i == 0)
    def _():
        issue(0)

    for par in range(2):
        @pl.when(jnp.logical_and(i >= 1, i % 2 == par))
        def _(par=par):
            cur = 1 - par
            wait(cur)
            issue(par)
            moe = gate_ref[:, 0:1] * slots[cur][0][...] + gate_ref[:, 1:2] * slots[cur][1][...]
            o_ref[...] = _rms(x_ref[...] + moe, g_ref[...])

            if par == last % 2:
                @pl.when(i == last)
                def _():
                    wait(par)


def _combine(dest_flat, x2d, gate_t, g, yb, *, tt, row_off):
    n, d = x2d.shape
    off = row_off // tt
    nt = n // tt
    prev = lambda i: jnp.maximum(i - 1, 0)
    return pl.pallas_call(
        functools.partial(_combine_kernel, last=nt), grid=(nt + 1,),
        in_specs=[pl.BlockSpec((2 * tt,), lambda i: (jnp.minimum(i, nt - 1) + off,), memory_space=pltpu.SMEM),
                  pl.BlockSpec((tt, d), lambda i: (prev(i), 0)),
                  pl.BlockSpec((tt, 2), lambda i: (prev(i) + off, 0)),
                  _full(g.shape),
                  pl.BlockSpec(memory_space=pl.ANY)],
        out_specs=pl.BlockSpec((tt, d), lambda i: (prev(i), 0)),
        out_shape=jax.ShapeDtypeStruct((n, d), F32),
        scratch_shapes=[pltpu.VMEM((tt, d), F32)] * 4 + [pltpu.SemaphoreType.DMA((2,))],
        compiler_params=_cparams(("arbitrary",)), name="combine")(dest_flat, x2d, gate_t, g, yb)


def _moe_final(xp, xs, mp, g_final):
    (n_p, d), n_s = xp.shape, xs.shape[0]
    n = n_p + n_s
    tt = min(ROUTE_ROWS, n_s)
    assert n_p % tt == 0 and n_s % tt == 0
    eid, gate, rank, cnt = _router(xp, xs, mp['g'], mp['wth'], mp['wtl'], mp['bias'], tt=tt)
    bm = EXPERT_ROWS
    counts = cnt[:, 0].astype(jnp.int32)
    padded = (counts + bm - 1) // bm * bm
    pend = jnp.cumsum(padded)
    pstart = pend - padded
    sel = eid[:, :, None] == jnp.arange(N_EXPERTS, dtype=jnp.int32)
    dest = rank + jnp.sum(jnp.where(sel, pstart, 0), axis=-1)
    dest = dest.T.reshape(-1)
    p_rows = (2 * n + N_EXPERTS * (bm - 1)) // bm * bm
    nblk = p_rows // bm
    block_start = jnp.arange(nblk, dtype=jnp.int32) * bm
    n_used = (pend[-1] // bm).astype(jnp.int32)
    block_e = jnp.sum((pend[None, :] <= block_start[:, None]).astype(jnp.int32), axis=1)
    last_e = jnp.sum((pend <= pend[-1] - 1).astype(jnp.int32))
    block_e = jnp.minimum(block_e, last_e).astype(jnp.int32)
    n_used = n_used.reshape(1)
    pad_lo = jnp.concatenate([pstart + counts, pend[-1:]])
    pad_hi = jnp.concatenate([pend, jnp.full((1,), p_rows, jnp.int32)])
    buf = _dispatch(pad_lo, pad_hi, dest, xp, xs, p_rows, tt=tt)
    yb = _experts(block_e, n_used, buf, mp['g'], mp['w1'], mp['w3'], mp['w2'], bm=bm)
    gate_t = gate.T
    y_p = _combine(dest, xp, gate_t, g_final, yb, tt=tt, row_off=0)
    y_s = _combine(dest, xs, gate_t, g_final, yb, tt=tt, row_off=n_p)
    return y_p, y_s


def _block_diag_pack(w, per):
    h, hd, _ = w.shape
    wg = w.reshape(h // per, per, hd, hd)
    eye = jnp.eye(per, dtype=w.dtype)
    return jnp.einsum('gpij,pq->gpiqj', wg, eye).reshape(h // per, per * hd, per * hd)


def kernel(x_prompt, x_sample, mem_prompt, state_conv_a, state_rglru_h, cache_mem_k, cache_mem_v, norm_mix, w_in, conv_a_w, conv_a_b, lru_w_r, lru_b_r, lru_w_i, lru_b_i, lru_lambda, v_norm_g, mlp_w_s, mlp_b_s, out_norm_a, out_norm_b, w_out, norm_mem_q, norm_mem_kv, w_mem_q, w_mem_k, w_mem_v, w_mem_o, norm_ffn, w_router_c, b_router_c, w_router_f, b_router_f, w_exp_1, w_exp_3, w_exp_2, norm_final):
    depth = w_in.shape[0]
    assert depth == 1
    l = 0
    bp, tp, d = x_prompt.shape
    bs, ts, _ = x_sample.shape
    d_a = conv_a_b.shape[-1]
    a_hd = lru_w_r.shape[-1]
    per = V7X_MXU_DIM // a_hd
    row = lambda v: v.reshape(1, -1)

    mix_p = dict(
        gin=row(norm_mix[l]), win=w_in[l].astype(BF16), cw=conv_a_w[l], cb=row(conv_a_b[l]),
        wr=_block_diag_pack(lru_w_r[l], per).astype(BF16), br=row(lru_b_r[l]),
        wi=_block_diag_pack(lru_w_i[l], per).astype(BF16), bi=row(lru_b_i[l]),
        lam=row(lru_lambda[l]), vg=row(v_norm_g[l]), ws=mlp_w_s[l], bst=mlp_b_s[l].T,
        goa=row(out_norm_a[l]), gob=row(out_norm_b[l]), wout=w_out[l].astype(BF16))

    conv0 = jnp.zeros((bp, CONV_W - 1, d_a), F32)
    h0 = jnp.zeros((bp, 1, d_a), F32)
    tt_p = min(MIX_ROWS, tp)
    x1p, conv_p, h_p = _mixer(x_prompt, conv0, h0, mix_p, bt=1, tt=tt_p, pos0=0, emit_vn=False)
    bt_mix = 8 if bs % 8 == 0 else 1
    x1s, conv_s, h_s, vn_s = _mixer(x_sample, state_conv_a[l], state_rglru_h[l][:, None, :], mix_p,
                                    bt=bt_mix, tt=ts, pos0=PAST_LEN, emit_vn=True)

    mt = mem_prompt.shape[1]
    k_p, v_p, kb_p, vb_p = _mem_kv(mem_prompt.reshape(bp * mt, d), row(norm_mem_kv[l]),
                                   w_mem_k[l].astype(BF16), w_mem_v[l].astype(BF16))
    gq = row(norm_mem_q[l])
    wq = w_mem_q[l].astype(BF16)
    wo = w_mem_o[l].astype(BF16)
    x2p = _attend(x1p, kb_p.reshape(bp, mt, d), vb_p.reshape(bp, mt, d), gq, wq, wo, bt=1,
                  tt=min(ATTEND_ROWS, tp))
    bt_s = 4 if bs % 4 == 0 else 1
    x2s = _attend(x1s, cache_mem_k[l].astype(BF16).reshape(bs, mt, d),
                  cache_mem_v[l].astype(BF16).reshape(bs, mt, d), gq, wq, wo, bt=bt_s, tt=ts)

    wt = jnp.concatenate([w_router_c[l].T, jnp.zeros((FINE_ROW0 - N_GROUPS, d), F32), w_router_f[l].T], axis=0)
    wth = wt.astype(BF16)
    wtl = (wt - wth.astype(F32)).astype(BF16)
    bias = jnp.concatenate([b_router_c[l], jnp.zeros((FINE_ROW0 - N_GROUPS,), F32), b_router_f[l]]).reshape(-1, 1)
    moe_p = dict(g=row(norm_ffn[l]), wth=wth, wtl=wtl, bias=bias, w1=w_exp_1[l], w3=w_exp_3[l], w2=w_exp_2[l])
    y_p, y_s = _moe_final(x2p, x2s, moe_p, row(norm_final))

    heads_b = v_norm_g.shape[1]
    hd_m = d // MEM_HEADS
    return (y_p.reshape(bp, tp, d), y_s.reshape(bs, ts, d),
            conv_p[None], h_p.reshape(1, bp, d_a),
            k_p.reshape(1, bp, mt, MEM_HEADS, hd_m), v_p.reshape(1, bp, mt, MEM_HEADS, hd_m),
            conv_s[None], h_s.reshape(1, bs, d_a),
            vn_s.reshape(1, bs, ts, heads_b, B_HD))
```

```python
import functools

import jax
import jax.numpy as jnp
from jax import lax
from jax.experimental import pallas as pl
from jax.experimental.pallas import tpu as pltpu

EPS = 1e-6
LRU_C = 8.0
CONV_W = 4
A_HEADS = 16
B_HD = 128
MLP_CHUNK = 128
MEM_HEADS = 4
N_GROUPS = 4
EXPERTS_PER_GROUP = 8
N_EXPERTS = N_GROUPS * EXPERTS_PER_GROUP
PAST_LEN = 1024

BF16 = jnp.bfloat16
F32 = jnp.float32

V7X_VMEM_BYTES = 64 * 1024 * 1024
V7X_MXU_DIM = 256
SUBLANES = 8
XA_ROW0 = SUBLANES
FINE_ROW0 = SUBLANES

MIX_ROWS = 256
ATTEND_ROWS = 512
ROUTE_ROWS = 512
EXPERT_ROWS = 512
ZERO_ROWS = 128
VMEM_LIMIT = V7X_VMEM_BYTES * 7 // 8


def _cparams(sem):
    return pltpu.CompilerParams(dimension_semantics=sem, vmem_limit_bytes=VMEM_LIMIT)


def _rms(x, g):
    return x * lax.rsqrt(jnp.mean(x * x, axis=-1, keepdims=True) + EPS) * g


def _dot(a, b):
    return jnp.dot(a, b, preferred_element_type=F32)


def _dot_nt(a, b):
    return lax.dot_general(a, b, (((1,), (1,)), ((), ())), preferred_element_type=F32)


def _full(shape):
    nd = len(shape)
    return pl.BlockSpec(shape, lambda *_: (0,) * nd)


def _gelu_proj(xn, w_ref, lo, width):
    cc = V7X_MXU_DIM
    return jnp.concatenate([jax.nn.gelu(_dot(xn, w_ref[:, lo + c * cc:lo + (c + 1) * cc]))
                            for c in range(width // cc)], axis=-1)


def _mixer_kernel(x_ref, conv0_ref, h0_ref, gin_ref, win_ref, cw_ref, cb_ref, wr_ref, br_ref,
                  wi_ref, bi_ref, lam_ref, vg_ref, ws_ref, bst_ref, goa_ref, gob_ref, wout_ref,
                  *rest, bt, tt, chunk, pos0, emit_vn):
    if emit_vn:
        y_ref, convo_ref, ho_ref, vn_ref = rest[:4]
        scratch = rest[4:]
    else:
        y_ref, convo_ref, ho_ref = rest[:3]
        vn_ref = None
        scratch = rest[3:]
    xa_buf, a_buf, u_buf, h_buf, z_buf, hc_ref, ycat = scratch
    t = pl.program_id(1)
    d = x_ref.shape[-1]
    d_a = a_buf.shape[-1]
    d_b = z_buf.shape[-1]
    m = bt * tt

    hist = XA_ROW0 - (CONV_W - 1)

    @pl.when(t == 0)
    def _():
        xa_buf[:, hist:XA_ROW0, :] = conv0_ref[...]
        hc_ref[...] = h0_ref[...]

    x = x_ref[...].reshape(m, d)
    xn = _rms(x, gin_ref[...]).astype(BF16)
    xa = _dot(xn, win_ref[:, 0:d_a])
    gga = _gelu_proj(xn, win_ref, d_a, d_a)
    gub = _gelu_proj(xn, win_ref, 2 * d_a, d_b)
    gv = _gelu_proj(xn, win_ref, 2 * d_a + d_b, d_b)

    xa_buf[:, XA_ROW0:XA_ROW0 + tt, :] = xa.reshape(bt, tt, d_a)
    xc = cb_ref[...][None]
    for k in range(CONV_W):
        xc = xc + xa_buf[:, hist + k:hist + k + tt, :] * cw_ref[k:k + 1, :][None]
    tail = xa_buf[:, tt + hist:tt + XA_ROW0, :]
    convo_ref[...] = tail
    xa_buf[:, hist:XA_ROW0, :] = tail
    xc = xc.reshape(m, d_a)

    xcb = xc.astype(BF16)
    gw = V7X_MXU_DIM
    r_pre = jnp.concatenate([_dot(xcb[:, g * gw:(g + 1) * gw], wr_ref[g]) for g in range(d_a // gw)], axis=-1)
    i_pre = jnp.concatenate([_dot(xcb[:, g * gw:(g + 1) * gw], wi_ref[g]) for g in range(d_a // gw)], axis=-1)
    r = jax.nn.sigmoid(r_pre + br_ref[...])
    ig = jax.nn.sigmoid(i_pre + bi_ref[...])
    nl = -lam_ref[...]
    softplus = jnp.maximum(nl, 0.0) + jnp.log1p(jnp.exp(-jnp.abs(nl)))
    log_a = -LRU_C * r * softplus
    a = jnp.exp(log_a)
    m2 = -jnp.tanh(log_a) * (1.0 + a * a)
    mult = jnp.where(m2 > 0.0, m2 * lax.rsqrt(m2), 0.0)
    if pos0 == 0:
        row = lax.broadcasted_iota(jnp.int32, (m, d_a), 0)
        is_first = jnp.logical_and(t == 0, (row % tt) == 0)
        mult = jnp.where(is_first, 1.0, mult)
    u = mult * (ig * xc)
    a_buf[...] = a.reshape(bt, tt, d_a)
    u_buf[...] = u.reshape(bt, tt, d_a)

    sub = lax.broadcasted_iota(jnp.int32, (SUBLANES, d_a), 0)
    for b in range(bt):
        h = hc_ref[b]
        for j in range(tt // SUBLANES):
            r0 = j * SUBLANES
            av = a_buf[b, r0:r0 + SUBLANES, :]
            uv = u_buf[b, r0:r0 + SUBLANES, :]
            for s in (1, 2, 4):
                a_sh = jnp.where(sub >= s, pltpu.roll(av, s, 0), 1.0)
                u_sh = jnp.where(sub >= s, pltpu.roll(uv, s, 0), 0.0)
                uv = av * u_sh + uv
                av = av * a_sh
            hv = av * h + uv
            h_buf[b, r0:r0 + SUBLANES, :] = hv
            h = hv[SUBLANES - 1:SUBLANES, :]
        hc_ref[b] = h
    ho_ref[...] = hc_ref[...]

    ya = h_buf[...].reshape(m, d_a) * gga
    ycat[:, 0:d_a] = _rms(ya, goa_ref[...]).astype(BF16)

    heads = d_b // B_HD
    ri = lax.broadcasted_iota(jnp.int32, (chunk, chunk), 0)
    ci = lax.broadcasted_iota(jnp.int32, (chunk, chunk), 1)
    for hh in range(heads):
        lo = hh * B_HD
        gvh = gv[:, lo:lo + B_HD]
        vn_h = gvh * lax.rsqrt(jnp.mean(gvh * gvh, axis=-1, keepdims=True) + EPS) * vg_ref[:, lo:lo + B_HD]
        if emit_vn:
            vn_ref[:, :, lo:lo + B_HD] = vn_h.reshape(bt, tt, B_HD)
        vn_hb = vn_h.astype(BF16)
        ws_h = jnp.where(ci <= ri, ws_ref[hh], 0.0).astype(BF16)
        bias = bst_ref[:, hh:hh + 1]
        for c in range(m // chunk):
            z = _dot(ws_h, vn_hb[c * chunk:(c + 1) * chunk, :]) + bias
            z_buf[c * chunk:(c + 1) * chunk, lo:lo + B_HD] = z
    yb = gub * z_buf[...]
    ycat[:, d_a:] = _rms(yb, gob_ref[...]).astype(BF16)

    y_ref[...] = (x + _dot(ycat[...], wout_ref[...])).reshape(bt, tt, d)


def _mixer(x, conv0, h0, p, *, bt, tt, pos0, emit_vn):
    nb, seq, d = x.shape
    d_a = conv0.shape[-1]
    d_b = p['vg'].shape[-1]
    chunk = min(tt, MLP_CHUNK)
    assert nb % bt == 0 and seq % tt == 0 and tt % chunk == 0 and tt % SUBLANES == 0
    grid = (nb // bt, seq // tt)
    weights = [p['gin'], p['win'], p['cw'], p['cb'], p['wr'], p['br'], p['wi'], p['bi'], p['lam'],
               p['vg'], p['ws'][:, :chunk, :chunk], p['bst'][:chunk], p['goa'], p['gob'], p['wout']]
    in_specs = [pl.BlockSpec((bt, tt, d), lambda b, t: (b, t, 0)),
                pl.BlockSpec((bt, CONV_W - 1, d_a), lambda b, t: (b, 0, 0)),
                pl.BlockSpec((bt, 1, d_a), lambda b, t: (b, 0, 0))]
    in_specs += [_full(w.shape) for w in weights]
    out_shape = [jax.ShapeDtypeStruct((nb, seq, d), F32),
                 jax.ShapeDtypeStruct((nb, CONV_W - 1, d_a), F32),
                 jax.ShapeDtypeStruct((nb, 1, d_a), F32)]
    out_specs = [pl.BlockSpec((bt, tt, d), lambda b, t: (b, t, 0)),
                 pl.BlockSpec((bt, CONV_W - 1, d_a), lambda b, t: (b, 0, 0)),
                 pl.BlockSpec((bt, 1, d_a), lambda b, t: (b, 0, 0))]
    if emit_vn:
        out_shape.append(jax.ShapeDtypeStruct((nb, seq, d_b), F32))
        out_specs.append(pl.BlockSpec((bt, tt, d_b), lambda b, t: (b, t, 0)))
    scratch = [pltpu.VMEM((bt, tt + XA_ROW0, d_a), F32), pltpu.VMEM((bt, tt, d_a), F32),
               pltpu.VMEM((bt, tt, d_a), F32), pltpu.VMEM((bt, tt, d_a), F32),
               pltpu.VMEM((bt * tt, d_b), F32), pltpu.VMEM((bt, 1, d_a), F32),
               pltpu.VMEM((bt * tt, d), BF16)]
    kern = functools.partial(_mixer_kernel, bt=bt, tt=tt, chunk=chunk, pos0=pos0, emit_vn=emit_vn)
    return pl.pallas_call(
        kern, grid=grid, in_specs=in_specs, out_specs=out_specs, out_shape=out_shape,
        scratch_shapes=scratch, compiler_params=_cparams(("arbitrary", "arbitrary")),
        name="mixer")(x, conv0, h0, *weights)


def _memkv_kernel(m_ref, g_ref, wk_ref, wv_ref, k_ref, v_ref, kb_ref, vb_ref):
    mn = _rms(m_ref[...], g_ref[...]).astype(BF16)
    k = _dot(mn, wk_ref[...])
    v = _dot(mn, wv_ref[...])
    hd = k_ref.shape[-1]
    for hh in range(k_ref.shape[1]):
        k_ref[:, hh, :] = k[:, hh * hd:(hh + 1) * hd]
        v_ref[:, hh, :] = v[:, hh * hd:(hh + 1) * hd]
    kb_ref[...] = k.astype(BF16)
    vb_ref[...] = v.astype(BF16)


def _mem_kv(mem2d, g, wk, wv, *, tm=256):
    n, d = mem2d.shape
    hd = d // MEM_HEADS
    f_spec = pl.BlockSpec((tm, MEM_HEADS, hd), lambda i: (i, 0, 0))
    b_spec = pl.BlockSpec((tm, d), lambda i: (i, 0))
    return pl.pallas_call(
        _memkv_kernel, grid=(n // tm,),
        in_specs=[pl.BlockSpec((tm, d), lambda i: (i, 0)), _full(g.shape), _full(wk.shape), _full(wv.shape)],
        out_specs=[f_spec, f_spec, b_spec, b_spec],
        out_shape=[jax.ShapeDtypeStruct((n, MEM_HEADS, hd), F32), jax.ShapeDtypeStruct((n, MEM_HEADS, hd), F32),
                   jax.ShapeDtypeStruct((n, d), BF16), jax.ShapeDtypeStruct((n, d), BF16)],
        compiler_params=_cparams(("arbitrary",)), name="mem_kv")(mem2d, g, wk, wv)


def _attend_kernel(x_ref, k_ref, v_ref, g_ref, wq_ref, wo_ref, y_ref, o_buf, *, bt, tt):
    d = x_ref.shape[-1]
    hd = d // MEM_HEADS
    x = x_ref[...]
    xn = _rms(x, g_ref[...]).astype(BF16)
    q = _dot(xn, wq_ref[...]).astype(BF16)
    scale = hd ** -0.5
    for b in range(bt):
        for hh in range(MEM_HEADS):
            qh = q[b * tt:(b + 1) * tt, hh * hd:(hh + 1) * hd]
            s = _dot_nt(qh, k_ref[b, :, hh * hd:(hh + 1) * hd]) * scale
            e = jnp.exp(s - jnp.max(s, axis=-1, keepdims=True))
            pr = e / jnp.sum(e, axis=-1, keepdims=True)
            o = _dot(pr.astype(BF16), v_ref[b, :, hh * hd:(hh + 1) * hd])
            o_buf[b * tt:(b + 1) * tt, hh * hd:(hh + 1) * hd] = o.astype(BF16)
    y_ref[...] = x + _dot(o_buf[...], wo_ref[...])


def _attend(x, k, v, g, wq, wo, *, bt, tt):
    nb, seq, d = x.shape
    mt = k.shape[1]
    m = bt * tt
    n = nb * seq
    assert n % m == 0 and (bt == 1 or tt == seq)
    kern = functools.partial(_attend_kernel, bt=bt, tt=tt)
    kv_spec = pl.BlockSpec((bt, mt, d), lambda i: ((i * m) // (bt * seq), 0, 0))
    return pl.pallas_call(
        kern, grid=(n // m,),
        in_specs=[pl.BlockSpec((m, d), lambda i: (i, 0)), kv_spec, kv_spec,
                  _full(g.shape), _full(wq.shape), _full(wo.shape)],
        out_specs=pl.BlockSpec((m, d), lambda i: (i, 0)),
        out_shape=jax.ShapeDtypeStruct((n, d), F32),
        scratch_shapes=[pltpu.VMEM((m, d), BF16)],
        compiler_params=_cparams(("arbitrary",)), name="attend")(x.reshape(n, d), k, v, g, wq, wo)


def _two_source_specs(tt, d, npb):
    return [pl.BlockSpec((tt, d), lambda i, *_: (jnp.minimum(i, npb - 1), 0)),
            pl.BlockSpec((tt, d), lambda i, *_: (jnp.maximum(i - npb, 0), 0))]


def _router_kernel(xp_ref, xs_ref, g_ref, wth_ref, wtl_ref, bias_ref, eid_ref, gate_ref, rank_ref, cnt_ref, *, npb):
    i = pl.program_id(0)
    tt = xp_ref.shape[0]

    @pl.when(i == 0)
    def _():
        cnt_ref[...] = jnp.zeros_like(cnt_ref)

    x = jnp.where(i < npb, xp_ref[...], xs_ref[...])
    xn = _rms(x, g_ref[...])
    xh = xn.astype(BF16)
    xl = (xn - xh.astype(F32)).astype(BF16)
    logits = (_dot_nt(wth_ref[...], xh) + _dot_nt(wth_ref[...], xl) + _dot_nt(wtl_ref[...], xh)) + bias_ref[...]
    lc = logits[0:N_GROUPS]
    ec = jnp.exp(lc - jnp.max(lc, axis=0, keepdims=True))
    pc = ec / jnp.sum(ec, axis=0, keepdims=True)
    p_grp = jnp.max(pc, axis=0, keepdims=True)
    gi = lax.broadcasted_iota(jnp.int32, pc.shape, 0)
    grp = jnp.min(jnp.where(pc == p_grp, gi, N_GROUPS), axis=0, keepdims=True)
    lf = jnp.zeros((EXPERTS_PER_GROUP, tt), F32)
    for gidx in range(N_GROUPS):
        lo = FINE_ROW0 + gidx * EXPERTS_PER_GROUP
        lf = lf + jnp.where(grp == gidx, logits[lo:lo + EXPERTS_PER_GROUP], 0.0)
    fi = lax.broadcasted_iota(jnp.int32, lf.shape, 0)
    t1 = jnp.max(lf, axis=0, keepdims=True)
    i1 = jnp.min(jnp.where(lf == t1, fi, EXPERTS_PER_GROUP), axis=0, keepdims=True)
    lf2 = jnp.where(fi == i1, -jnp.inf, lf)
    t2 = jnp.max(lf2, axis=0, keepdims=True)
    i2 = jnp.min(jnp.where(lf2 == t2, fi, EXPERTS_PER_GROUP), axis=0, keepdims=True)
    e2 = jnp.exp(t2 - t1)
    den = 1.0 + e2
    gate_ref[0:1, :] = p_grp * (1.0 / den)
    gate_ref[1:2, :] = p_grp * (e2 / den)
    eid1 = grp * EXPERTS_PER_GROUP + i1
    eid2 = grp * EXPERTS_PER_GROUP + i2
    eid_ref[0:1, :] = eid1
    eid_ref[1:2, :] = eid2

    ei = lax.broadcasted_iota(jnp.int32, (N_EXPERTS, tt), 0)
    oh1 = ei == eid1
    oh2 = ei == eid2
    oh = jnp.where(jnp.logical_or(oh1, oh2), 1.0, 0.0)
    tr = lax.broadcasted_iota(jnp.int32, (tt, tt), 0)
    tc = lax.broadcasted_iota(jnp.int32, (tt, tt), 1)
    upper = jnp.where(tr < tc, 1.0, 0.0).astype(BF16)
    run = cnt_ref[:, 0:1]
    before = _dot(oh.astype(BF16), upper) + run
    rank_ref[0:1, :] = jnp.sum(jnp.where(oh1, before, 0.0), axis=0, keepdims=True).astype(jnp.int32)
    rank_ref[1:2, :] = jnp.sum(jnp.where(oh2, before, 0.0), axis=0, keepdims=True).astype(jnp.int32)
    cnt_ref[...] = cnt_ref[...] + jnp.sum(oh, axis=1, keepdims=True)


def _router(xp, xs, g, wth, wtl, bias, *, tt):
    (n_p, d), n_s = xp.shape, xs.shape[0]
    npb, n = n_p // tt, n_p + n_s
    tok = pl.BlockSpec((2, tt), lambda i: (0, i))
    return pl.pallas_call(
        functools.partial(_router_kernel, npb=npb), grid=(n // tt,),
        in_specs=_two_source_specs(tt, d, npb) + [_full(g.shape), _full(wth.shape), _full(wtl.shape),
                                                  _full(bias.shape)],
        out_specs=[tok, tok, tok, pl.BlockSpec((N_EXPERTS, 128), lambda i: (0, 0))],
        out_shape=[jax.ShapeDtypeStruct((2, n), jnp.int32),
                   jax.ShapeDtypeStruct((2, n), F32),
                   jax.ShapeDtypeStruct((2, n), jnp.int32),
                   jax.ShapeDtypeStruct((N_EXPERTS, 128), F32)],
        compiler_params=_cparams(("arbitrary",)), name="router")(xp, xs, g, wth, wtl, bias)


def _dispatch_kernel(zlo_ref, zhi_ref, dest_ref, xp_hbm, xs_hbm, buf_ref, zero_buf, xbuf, fsem, ssem, zsem,
                     *, npb, nt):
    i = pl.program_id(0)
    tt = xbuf.shape[1]

    def fetch(t, slot):
        @pl.when(t < npb)
        def _():
            pltpu.make_async_copy(xp_hbm.at[pl.ds(pl.multiple_of(t * tt, tt), tt), :],
                                  xbuf.at[slot], fsem.at[slot]).start()

        @pl.when(t >= npb)
        def _():
            pltpu.make_async_copy(xs_hbm.at[pl.ds(pl.multiple_of((t - npb) * tt, tt), tt), :],
                                  xbuf.at[slot], fsem.at[slot]).start()

    def fetch_wait(slot):
        pltpu.make_async_copy(xp_hbm.at[pl.ds(0, tt), :], xbuf.at[slot], fsem.at[slot]).wait()

    def scatter_wait(slot):
        for k in range(2):
            pltpu.make_async_copy(xbuf.at[slot], buf_ref.at[pl.ds(0, tt), :], ssem.at[slot]).wait()

    @pl.when(i == 0)
    def _():
        zero_buf[...] = jnp.zeros_like(zero_buf)
        zc = zero_buf.shape[0]

        def zcopy(chunk):
            return pltpu.make_async_copy(zero_buf, buf_ref.at[pl.ds(pl.multiple_of(chunk * zc, zc), zc), :], zsem)

        def zstart(chunk, c):
            zcopy(chunk).start()
            return c

        def zwait(chunk, c):
            zcopy(chunk).wait()
            return c

        for e in range(zlo_ref.shape[0]):
            lax.fori_loop(zlo_ref[e], zhi_ref[e], zstart, 0)
        for e in range(zlo_ref.shape[0]):
            lax.fori_loop(zlo_ref[e], zhi_ref[e], zwait, 0)
        fetch(0, 0)

    slot = i % 3
    nxt = (i + 1) % 3

    @pl.when(i >= 2)
    def _():
        scatter_wait(nxt)

    @pl.when(i + 1 < nt)
    def _():
        fetch(i + 1, nxt)

    fetch_wait(slot)
    x_ref = xbuf.at[slot]
    s_sem = ssem.at[slot]

    def group(gidx, c):
        r0 = pl.multiple_of(gidx * SUBLANES, SUBLANES)
        rows = x_ref.at[pl.ds(r0, SUBLANES), :]
        for j in range(SUBLANES):
            for k in range(2):
                pltpu.make_async_copy(rows.at[pl.ds(j, 1), :],
                                      buf_ref.at[pl.ds(dest_ref[2 * r0 + (2 * j + k)], 1), :],
                                      s_sem).start(priority=k)
        return c

    lax.fori_loop(0, tt // SUBLANES, group, 0)

    @pl.when(i == nt - 1)
    def _():
        if nt > 1:
            scatter_wait((i + 2) % 3)
        scatter_wait(slot)


def _dispatch(pad_lo, pad_hi, dest, xp, xs, p_rows, *, tt):
    (n_p, d), n_s = xp.shape, xs.shape[0]
    npb, n = n_p // tt, n_p + n_s
    nt = n // tt
    zlo = (pad_lo // ZERO_ROWS).astype(jnp.int32)
    zhi = (pad_hi // ZERO_ROWS).astype(jnp.int32)
    gs = pltpu.PrefetchScalarGridSpec(
        num_scalar_prefetch=2, grid=(nt,),
        in_specs=[pl.BlockSpec((2 * tt,), lambda i, *_: (i,), memory_space=pltpu.SMEM),
                  pl.BlockSpec(memory_space=pl.ANY), pl.BlockSpec(memory_space=pl.ANY)],
        out_specs=pl.BlockSpec(memory_space=pl.ANY),
        scratch_shapes=[pltpu.VMEM((ZERO_ROWS, d), F32), pltpu.VMEM((3, tt, d), F32),
                        pltpu.SemaphoreType.DMA((3,)), pltpu.SemaphoreType.DMA((3,)),
                        pltpu.SemaphoreType.DMA(())])
    return pl.pallas_call(
        functools.partial(_dispatch_kernel, npb=npb, nt=nt), grid_spec=gs,
        out_shape=jax.ShapeDtypeStruct((p_rows, d), F32),
        compiler_params=_cparams(("arbitrary",)), name="dispatch")(zlo, zhi, dest, xp, xs)


def _experts_kernel(be_ref, nu_ref, bv_ref, x_ref, g_ref, w1_ref, w3_ref, w2_ref, o_ref, w1b, w3b, w2b):
    i = pl.program_id(0)
    bm = x_ref.shape[0]
    half = bm // 2
    used = i < nu_ref[0]

    @pl.when(used)
    def _():
        prev = be_ref[jnp.maximum(i - 1, 0)]

        @pl.when(jnp.logical_or(i == 0, be_ref[i] != prev))
        def _():
            w1b[...] = w1_ref[0].astype(BF16)
            w3b[...] = w3_ref[0].astype(BF16)
            w2b[...] = w2_ref[0].astype(BF16)

    def evaluate(rows):
        x = x_ref[0:rows, :]
        xg = (x * g_ref[...]).astype(BF16)
        scale = lax.rsqrt(jnp.mean(x * x, axis=-1, keepdims=True) + EPS)
        h1 = _dot(xg, w1b[...]) * scale
        h3 = _dot(xg, w3b[...]) * scale
        hd = (h1 * jax.nn.sigmoid(h1) * h3).astype(BF16)
        o_ref[0:rows, :] = _dot(hd, w2b[...])
        if rows < bm:
            o_ref[rows:, :] = jnp.zeros((bm - rows, o_ref.shape[1]), o_ref.dtype)

    @pl.when(jnp.logical_and(used, bv_ref[i] > half))
    def _():
        evaluate(bm)

    @pl.when(jnp.logical_and(used, bv_ref[i] <= half))
    def _():
        evaluate(half)

    @pl.when(i >= nu_ref[0])
    def _():
        o_ref[...] = jnp.zeros_like(o_ref)


def _experts(block_e, n_used, block_valid, buf, g, w1, w3, w2, *, bm):
    p_rows, d = buf.shape
    ne, _, de = w1.shape
    nblk = p_rows // bm
    gs = pltpu.PrefetchScalarGridSpec(
        num_scalar_prefetch=3, grid=(nblk,),
        in_specs=[pl.BlockSpec((bm, d), lambda i, be, nu, bv: (jnp.minimum(i, nu[0] - 1), 0)),
                  pl.BlockSpec(g.shape, lambda i, be, nu, bv: (0, 0)),
                  pl.BlockSpec((1, d, de), lambda i, be, nu, bv: (be[i], 0, 0)),
                  pl.BlockSpec((1, d, de), lambda i, be, nu, bv: (be[i], 0, 0)),
                  pl.BlockSpec((1, de, d), lambda i, be, nu, bv: (be[i], 0, 0))],
        out_specs=pl.BlockSpec((bm, d), lambda i, be, nu, bv: (i, 0)),
        scratch_shapes=[pltpu.VMEM((d, de), BF16), pltpu.VMEM((d, de), BF16), pltpu.VMEM((de, d), BF16)])
    return pl.pallas_call(
        _experts_kernel, grid_spec=gs,
        out_shape=jax.ShapeDtypeStruct((p_rows, d), F32),
        compiler_params=_cparams(("arbitrary",)), name="experts")(block_e, n_used, block_valid, buf, g, w1, w3, w2)


def _combine_kernel(dest_ref, x_ref, gate_ref, g_ref, yb_ref, o_ref, a0, b0, a1, b1, sems, *, last):
    i = pl.program_id(0)
    tt = x_ref.shape[0]
    slots = ((a0, b0), (a1, b1))

    def issue(slot):
        for r in range(tt):
            for k in range(2):
                pltpu.make_async_copy(yb_ref.at[pl.ds(dest_ref[2 * r + k], 1), :],
                                      slots[slot][k].at[pl.ds(r, 1), :],
                                      sems.at[slot]).start(priority=k)

    def wait(slot):
        for k in range(2):
            pltpu.make_async_copy(yb_ref.at[pl.ds(0, tt), :], slots[slot][k], sems.at[slot]).wait()

    @pl.when(i == 0)
    def _():
        issue(0)

    for par in range(2):
        @pl.when(jnp.logical_and(i >= 1, i % 2 == par))
        def _(par=par):
            cur = 1 - par
            wait(cur)
            issue(par)
            moe = gate_ref[:, 0:1] * slots[cur][0][...] + gate_ref[:, 1:2] * slots[cur][1][...]
            o_ref[...] = _rms(x_ref[...] + moe, g_ref[...])

            if par == last % 2:
                @pl.when(i == last)
                def _():
                    wait(par)


def _combine(dest_flat, x2d, gate_t, g, yb, *, tt, row_off):
    n, d = x2d.shape
    off = row_off // tt
    nt = n // tt
    prev = lambda i: jnp.maximum(i - 1, 0)
    return pl.pallas_call(
        functools.partial(_combine_kernel, last=nt), grid=(nt + 1,),
        in_specs=[pl.BlockSpec((2 * tt,), lambda i: (jnp.minimum(i, nt - 1) + off,), memory_space=pltpu.SMEM),
                  pl.BlockSpec((tt, d), lambda i: (prev(i), 0)),
                  pl.BlockSpec((tt, 2), lambda i: (prev(i) + off, 0)),
                  _full(g.shape),
                  pl.BlockSpec(memory_space=pl.ANY)],
        out_specs=pl.BlockSpec((tt, d), lambda i: (prev(i), 0)),
        out_shape=jax.ShapeDtypeStruct((n, d), F32),
        scratch_shapes=[pltpu.VMEM((tt, d), F32)] * 4 + [pltpu.SemaphoreType.DMA((2,))],
        compiler_params=_cparams(("arbitrary",)), name="combine")(dest_flat, x2d, gate_t, g, yb)


def _moe_final(xp, xs, mp, g_final):
    (n_p, d), n_s = xp.shape, xs.shape[0]
    n = n_p + n_s
    tt = min(ROUTE_ROWS, n_s)
    assert n_p % tt == 0 and n_s % tt == 0
    eid, gate, rank, cnt = _router(xp, xs, mp['g'], mp['wth'], mp['wtl'], mp['bias'], tt=tt)
    bm = EXPERT_ROWS
    counts = cnt[:, 0].astype(jnp.int32)
    padded = (counts + bm - 1) // bm * bm
    pend = jnp.cumsum(padded)
    pstart = pend - padded
    sel = eid[:, :, None] == jnp.arange(N_EXPERTS, dtype=jnp.int32)
    dest = rank + jnp.sum(jnp.where(sel, pstart, 0), axis=-1)
    dest = dest.T.reshape(-1)
    p_rows = (2 * n + N_EXPERTS * (bm - 1)) // bm * bm
    nblk = p_rows // bm
    block_start = jnp.arange(nblk, dtype=jnp.int32) * bm
    n_used = (pend[-1] // bm).astype(jnp.int32)
    block_e = jnp.sum((pend[None, :] <= block_start[:, None]).astype(jnp.int32), axis=1)
    last_e = jnp.sum((pend <= pend[-1] - 1).astype(jnp.int32))
    block_e = jnp.minimum(block_e, last_e).astype(jnp.int32)
    n_used = n_used.reshape(1)
    pad_lo = jnp.concatenate([pstart + counts, pend[-1:]])
    pad_hi = jnp.concatenate([pend, jnp.full((1,), p_rows, jnp.int32)])
    buf = _dispatch(pad_lo, pad_hi, dest, xp, xs, p_rows, tt=tt)
    row_end = jnp.sum(jnp.where(block_e[:, None] == jnp.arange(N_EXPERTS, dtype=jnp.int32), pstart + counts, 0), axis=1)
    block_valid = jnp.clip(row_end - block_start, 0, bm).astype(jnp.int32)
    yb = _experts(block_e, n_used, block_valid, buf, mp['g'], mp['w1'], mp['w3'], mp['w2'], bm=bm)
    gate_t = gate.T
    y_p = _combine(dest, xp, gate_t, g_final, yb, tt=tt, row_off=0)
    y_s = _combine(dest, xs, gate_t, g_final, yb, tt=tt, row_off=n_p)
    return y_p, y_s


def _block_diag_pack(w, per):
    h, hd, _ = w.shape
    wg = w.reshape(h // per, per, hd, hd)
    eye = jnp.eye(per, dtype=w.dtype)
    return jnp.einsum('gpij,pq->gpiqj', wg, eye).reshape(h // per, per * hd, per * hd)


def kernel(x_prompt, x_sample, mem_prompt, state_conv_a, state_rglru_h, cache_mem_k, cache_mem_v, norm_mix, w_in, conv_a_w, conv_a_b, lru_w_r, lru_b_r, lru_w_i, lru_b_i, lru_lambda, v_norm_g, mlp_w_s, mlp_b_s, out_norm_a, out_norm_b, w_out, norm_mem_q, norm_mem_kv, w_mem_q, w_mem_k, w_mem_v, w_mem_o, norm_ffn, w_router_c, b_router_c, w_router_f, b_router_f, w_exp_1, w_exp_3, w_exp_2, norm_final):
    depth = w_in.shape[0]
    assert depth == 1
    l = 0
    bp, tp, d = x_prompt.shape
    bs, ts, _ = x_sample.shape
    d_a = conv_a_b.shape[-1]
    a_hd = lru_w_r.shape[-1]
    per = V7X_MXU_DIM // a_hd
    row = lambda v: v.reshape(1, -1)

    mix_p = dict(
        gin=row(norm_mix[l]), win=w_in[l].astype(BF16), cw=conv_a_w[l], cb=row(conv_a_b[l]),
        wr=_block_diag_pack(lru_w_r[l], per).astype(BF16), br=row(lru_b_r[l]),
        wi=_block_diag_pack(lru_w_i[l], per).astype(BF16), bi=row(lru_b_i[l]),
        lam=row(lru_lambda[l]), vg=row(v_norm_g[l]), ws=mlp_w_s[l], bst=mlp_b_s[l].T,
        goa=row(out_norm_a[l]), gob=row(out_norm_b[l]), wout=w_out[l].astype(BF16))

    conv0 = jnp.zeros((bp, CONV_W - 1, d_a), F32)
    h0 = jnp.zeros((bp, 1, d_a), F32)
    tt_p = min(MIX_ROWS, tp)
    x1p, conv_p, h_p = _mixer(x_prompt, conv0, h0, mix_p, bt=1, tt=tt_p, pos0=0, emit_vn=False)
    bt_mix = 8 if bs % 8 == 0 else 1
    x1s, conv_s, h_s, vn_s = _mixer(x_sample, state_conv_a[l], state_rglru_h[l][:, None, :], mix_p,
                                    bt=bt_mix, tt=ts, pos0=PAST_LEN, emit_vn=True)

    mt = mem_prompt.shape[1]
    k_p, v_p, kb_p, vb_p = _mem_kv(mem_prompt.reshape(bp * mt, d), row(norm_mem_kv[l]),
                                   w_mem_k[l].astype(BF16), w_mem_v[l].astype(BF16))
    gq = row(norm_mem_q[l])
    wq = w_mem_q[l].astype(BF16)
    wo = w_mem_o[l].astype(BF16)
    x2p = _attend(x1p, kb_p.reshape(bp, mt, d), vb_p.reshape(bp, mt, d), gq, wq, wo, bt=1,
                  tt=min(ATTEND_ROWS, tp))
    bt_s = 4 if bs % 4 == 0 else 1
    x2s = _attend(x1s, cache_mem_k[l].astype(BF16).reshape(bs, mt, d),
                  cache_mem_v[l].astype(BF16).reshape(bs, mt, d), gq, wq, wo, bt=bt_s, tt=ts)

    wt = jnp.concatenate([w_router_c[l].T, jnp.zeros((FINE_ROW0 - N_GROUPS, d), F32), w_router_f[l].T], axis=0)
    wth = wt.astype(BF16)
    wtl = (wt - wth.astype(F32)).astype(BF16)
    bias = jnp.concatenate([b_router_c[l], jnp.zeros((FINE_ROW0 - N_GROUPS,), F32), b_router_f[l]]).reshape(-1, 1)
    moe_p = dict(g=row(norm_ffn[l]), wth=wth, wtl=wtl, bias=bias, w1=w_exp_1[l], w3=w_exp_3[l], w2=w_exp_2[l])
    y_p, y_s = _moe_final(x2p, x2s, moe_p, row(norm_final))

    heads_b = v_norm_g.shape[1]
    hd_m = d // MEM_HEADS
    return (y_p.reshape(bp, tp, d), y_s.reshape(bs, ts, d),
            conv_p[None], h_p.reshape(1, bp, d_a),
            k_p.reshape(1, bp, mt, MEM_HEADS, hd_m), v_p.reshape(1, bp, mt, MEM_HEADS, hd_m),
            conv_s[None], h_s.reshape(1, bs, d_a),
            vn_s.reshape(1, bs, ts, heads_b, B_HD))
```
